```python
import math
import jax, jax.numpy as jnp
from jax import lax
import numpy as np

D_MODEL = 2048
BATCH = 2
SEQ = 4096
DEPTH = 4
DEC_BATCH = 8
DEC_SEQ = 1
PAST_LEN = 16384
PAGE_SIZE = 128

HD = 64
H_A = 6
DA = H_A * 2 * HD
G_B = 8
HD_B = 64
DB_W = G_B * HD_B
CHUNK = 128
H_C = 12
DC = H_C * HD
MIX = DA + DB_W + DC
ROT_DIM = HD // 4
ROPE_THETA = 500000.0
Q_BLOCK = 128
N_GROUPS = 4
N_EXP_PER_GROUP = 4
N_EXP = N_GROUPS * N_EXP_PER_GROUP
D_EXP = 512
TOP_K = 2
EPS = 1e-6
N_IN = 3 * DA + 2 * DB_W + 3 * DC + H_C
_OFF = (DA, 2 * DA, 3 * DA, 3 * DA + DB_W, 3 * DA + 2 * DB_W, 3 * DA + 2 * DB_W + DC,
        3 * DA + 2 * DB_W + 2 * DC, 3 * DA + 2 * DB_W + 3 * DC)

kernel_name = "hymba_style_diff_sgu_fox_hmoe_step"


def _rmsnorm(x, g):
    xf = x.astype(jnp.float32)
    y = xf * lax.rsqrt(jnp.mean(xf * xf, axis=-1, keepdims=True) + EPS)
    return (y * g.astype(jnp.float32)).astype(x.dtype)


def _rope(x, pos):
    half = ROT_DIM // 2
    inv = ROPE_THETA ** (-jnp.arange(0, ROT_DIM, 2, dtype=jnp.float32) / ROT_DIM)
    ang = pos.astype(jnp.float32)[:, None] * inv[None, :]
    shp = (ang.shape[0],) + (1,) * (x.ndim - 3) + (half,)
    cos, sin = jnp.cos(ang).reshape(shp), jnp.sin(ang).reshape(shp)
    xr = x[..., :ROT_DIM].astype(jnp.float32)
    x1, x2 = xr[..., :half], xr[..., half:]
    rot = jnp.concatenate([x1 * cos - x2 * sin, x2 * cos + x1 * sin], axis=-1)
    return jnp.concatenate([rot.astype(x.dtype), x[..., ROT_DIM:]], axis=-1)


def _project(h, w_in, b_f, g_sgu, pos):
    b_, s_ = h.shape[0], h.shape[1]
    z = jnp.einsum('bsd,dn->bsn', h, w_in)
    qa, ka, va, ub, vb, qc, kc, vc, fl = jnp.split(z, _OFF, axis=-1)
    qa = _rope(qa.reshape(b_, s_, H_A, 2, HD), pos)
    ka = _rope(ka.reshape(b_, s_, H_A, 2, HD), pos)
    va = va.reshape(b_, s_, H_A, 2 * HD)
    ub = jax.nn.gelu(ub)
    vb = _rmsnorm(jax.nn.gelu(vb), g_sgu).reshape(b_, s_, G_B, HD_B)
    qc = qc.reshape(b_, s_, H_C, HD)
    kc = kc.reshape(b_, s_, H_C, HD)
    vc = vc.reshape(b_, s_, H_C, HD)
    logf = jax.nn.log_sigmoid((fl + b_f).astype(jnp.float32))
    return qa, ka, va, ub, vb, qc, kc, vc, logf


def _diff_lambda(dl, lam_init):
    dl = dl.astype(jnp.float32)
    return jnp.exp(jnp.sum(dl[0] * dl[1])) - jnp.exp(jnp.sum(dl[2] * dl[3])) + lam_init


def _diff_core(q, k, v, lam, qpos, kpos):
    s = jnp.einsum('bqhmd,bkhmd->bhmqk', q, k).astype(jnp.float32) * (HD ** -0.5)
    s = jnp.where(kpos[None, :] <= qpos[:, None], s, -jnp.inf)
    p = jax.nn.softmax(s, axis=-1)
    a = p[:, :, 0] - lam * p[:, :, 1]
    return jnp.einsum('bhqk,bkhe->bqhe', a.astype(v.dtype), v)


def _fox_core(q, k, v, dq, dk, qpos, kpos):
    s = jnp.einsum('bqhd,bkhd->bhqk', q, k).astype(jnp.float32) * (HD ** -0.5)
    s = s + (jnp.swapaxes(dq, 1, 2)[..., :, None] - jnp.swapaxes(dk, 1, 2)[..., None, :])
    s = jnp.where(kpos[None, :] <= qpos[:, None], s, -jnp.inf)
    p = jax.nn.softmax(s, axis=-1)
    return jnp.einsum('bhqk,bkhd->bqhd', p.astype(v.dtype), v)


def _to_blocks(a, nb):
    return jnp.moveaxis(a.reshape((a.shape[0], nb, Q_BLOCK) + a.shape[2:]), 1, 0)


def _from_blocks(a):
    a = jnp.moveaxis(a, 0, 1)
    return a.reshape((a.shape[0], a.shape[1] * a.shape[2]) + a.shape[3:])


def _diff_attn_prompt(q, k, v, lam):
    s_ = q.shape[1]
    nb = s_ // Q_BLOCK
    pos = jnp.arange(s_)
    out = lax.map(lambda a: _diff_core(a[0], k, v, lam, a[1], pos),
                  (_to_blocks(q, nb), pos.reshape(nb, Q_BLOCK)))
    return _from_blocks(out)


def _fox_prompt(q, k, v, logf):
    s_ = q.shape[1]
    nb = s_ // Q_BLOCK
    pos = jnp.arange(s_)
    d = jnp.cumsum(logf, axis=1)
    out = lax.map(lambda a: _fox_core(a[0], k, v, a[1], d, a[2], pos),
                  (_to_blocks(q, nb), _to_blocks(d, nb), pos.reshape(nb, Q_BLOCK)))
    return _from_blocks(out)


def _sgu_prompt(u, v, w_s, b_s):
    b_, s_ = u.shape[0], u.shape[1]
    vc = v.reshape(b_, s_ // CHUNK, CHUNK, G_B, HD_B)
    mix = jnp.einsum('gts,bcsgd->bctgd', jnp.tril(w_s), vc) + jnp.swapaxes(b_s, 0, 1)[:, :, None]
    return u * mix.reshape(b_, s_, DB_W)


def _sgu_sample(u, v, w_s, b_s):
    b_, t_ = u.shape[0], u.shape[1]
    w = jnp.tril(w_s)[:, :t_, :t_]
    mix = jnp.einsum('gts,bsgd->btgd', w, v) + jnp.swapaxes(b_s[:, :t_], 0, 1)[:, :, None]
    return u * mix.reshape(b_, t_, DB_W)


def _mix_out(oa, ob, oc, g_subln, lam_init, g_out_b, g_out_c, w_out):
    b_, s_ = oa.shape[0], oa.shape[1]
    oa = (_rmsnorm(oa, g_subln) * (1.0 - lam_init)).reshape(b_, s_, DA)
    ob = _rmsnorm(ob, g_out_b)
    oc = _rmsnorm(oc.reshape(b_, s_, DC), g_out_c)
    return jnp.einsum('bsm,md->bsd', jnp.concatenate([oa, ob, oc], axis=-1), w_out)


def _moe(h, w_rg, b_rg, w_re, b_re, w_gu, w_down):
    shp = h.shape
    x = h.reshape(-1, shp[-1])
    gl = (x @ w_rg + b_rg).astype(jnp.float32)
    grp = jnp.argmax(gl, axis=-1)
    p_grp = jnp.take_along_axis(jax.nn.softmax(gl, axis=-1), grp[:, None], axis=-1)
    el = (x @ w_re + b_re).astype(jnp.float32).reshape(-1, N_GROUPS, N_EXP_PER_GROUP)
    el = jnp.take_along_axis(el, grp[:, None, None], axis=1)[:, 0]
    tv, ti = lax.top_k(el, TOP_K)
    tw = jax.nn.softmax(tv, axis=-1) * p_grp
    eid = grp[:, None] * N_EXP_PER_GROUP + ti
    gate = jnp.sum(jax.nn.one_hot(eid, N_EXP, dtype=jnp.float32) * tw[..., None], axis=1)
    hu = jnp.einsum('nd,edf->nef', x, w_gu)
    a, b = jnp.split(hu, 2, axis=-1)
    hid = jax.nn.silu(a) * b * gate.astype(x.dtype)[..., None]
    return jnp.einsum('nef,efd->nd', hid, w_down).reshape(shp)


def _layer(x, c, mixer, w_ada_l, b_ada_l, g_norm_l, moe_w):
    mod = jax.nn.silu(c) @ w_ada_l + b_ada_l
    sh1, sc1, ga1, sh2, sc2, ga2 = jnp.split(mod[:, None, :], 6, axis=-1)
    m, new = mixer(_rmsnorm(x, g_norm_l[0]) * (1 + sc1) + sh1)
    x = x + ga1 * _rmsnorm(m, g_norm_l[1])
    h = _rmsnorm(x, g_norm_l[2]) * (1 + sc2) + sh2
    x = x + ga2 * _rmsnorm(_moe(h, *moe_w), g_norm_l[3])
    return x, new


def setup_inputs(seed: int = 0) -> dict:
    key = jax.random.key(seed)
    ks = jax.random.split(key, 32)
    f32 = jnp.float32
    n_pages = PAST_LEN // PAGE_SIZE
    n_pool = (5 * DEC_BATCH * n_pages) // 4
    nrm = lambda k, s, sc: jax.random.normal(k, s, f32) * sc
    page_table = jax.random.permutation(ks[10], n_pool)[:DEC_BATCH * n_pages]
    page_table = page_table.reshape(DEC_BATCH, n_pages).astype(jnp.int32)
    return {
        'x_prompt': nrm(ks[0], (BATCH, SEQ, D_MODEL), 1.0),
        'x_sample': nrm(ks[1], (DEC_BATCH, DEC_SEQ, D_MODEL), 1.0),
        'c_prompt': nrm(ks[2], (BATCH, D_MODEL), 1.0),
        'c_sample': nrm(ks[3], (DEC_BATCH, D_MODEL), 1.0),
        'cache_a_k': nrm(ks[4], (DEPTH, n_pool, PAGE_SIZE, H_A, 2 * HD), 1.0),
        'cache_a_v': nrm(ks[5], (DEPTH, n_pool, PAGE_SIZE, H_A, 2 * HD), 1.0),
        'cache_c_k': nrm(ks[6], (DEPTH, n_pool, PAGE_SIZE, H_C, HD), 1.0),
        'cache_c_v': nrm(ks[7], (DEPTH, n_pool, PAGE_SIZE, H_C, HD), 1.0),
        'cache_c_logf': jax.nn.log_sigmoid(2.0 + nrm(ks[8], (DEPTH, n_pool, PAGE_SIZE, H_C), 1.0)),
        'page_table': page_table,
        'w_ada': nrm(ks[11], (DEPTH, D_MODEL, 6 * D_MODEL), 0.3 * D_MODEL ** -0.5),
        'b_ada': nrm(ks[12], (DEPTH, 6 * D_MODEL), 0.01),
        'g_norm': 1.0 + nrm(ks[13], (DEPTH, 4, D_MODEL), 0.05),
        'w_in': nrm(ks[14], (DEPTH, D_MODEL, N_IN), D_MODEL ** -0.5),
        'b_f': 2.0 + nrm(ks[15], (DEPTH, H_C), 0.1),
        'g_sgu': 1.0 + nrm(ks[16], (DEPTH, DB_W), 0.05),
        'w_s': nrm(ks[17], (DEPTH, G_B, CHUNK, CHUNK), CHUNK ** -0.5),
        'b_s': 1.0 + nrm(ks[18], (DEPTH, G_B, CHUNK), 0.1),
        'diff_lambda': nrm(ks[19], (DEPTH, 4, HD), 0.1),
        'g_subln': 1.0 + nrm(ks[20], (DEPTH, 2 * HD), 0.05),
        'g_out_b': 1.0 + nrm(ks[21], (DEPTH, DB_W), 0.05),
        'g_out_c': 1.0 + nrm(ks[22], (DEPTH, DC), 0.05),
        'w_out': nrm(ks[23], (DEPTH, MIX, D_MODEL), MIX ** -0.5),
        'w_rg': nrm(ks[24], (DEPTH, D_MODEL, N_GROUPS), D_MODEL ** -0.5),
        'b_rg': nrm(ks[25], (DEPTH, N_GROUPS), 0.01),
        'w_re': nrm(ks[26], (DEPTH, D_MODEL, N_EXP), D_MODEL ** -0.5),
        'b_re': nrm(ks[27], (DEPTH, N_EXP), 0.01),
        'w_gu': nrm(ks[28], (DEPTH, N_EXP, D_MODEL, 2 * D_EXP), D_MODEL ** -0.5),
        'w_down': nrm(ks[29], (DEPTH, N_EXP, D_EXP, D_MODEL), D_EXP ** -0.5),
    }


def reference(x_prompt, x_sample, c_prompt, c_sample, cache_a_k, cache_a_v, cache_c_k, cache_c_v,
              cache_c_logf, page_table, w_ada, b_ada, g_norm, w_in, b_f, g_sgu, w_s, b_s,
              diff_lambda, g_subln, g_out_b, g_out_c, w_out, w_rg, b_rg, w_re, b_re, w_gu, w_down):
    n_dec, dec_len = x_sample.shape[0], x_sample.shape[1]
    past = page_table.shape[1] * cache_a_k.shape[2]
    pos_p = jnp.arange(x_prompt.shape[1])
    pos_s = past + jnp.arange(dec_len)
    kpos_s = jnp.arange(past + dec_len)
    xp, xs = x_prompt, x_sample
    a_k_p, a_v_p, c_k_p, c_v_p, c_lf_p = [], [], [], [], []
    a_k_s, a_v_s, c_k_s, c_v_s, c_lf_s, sgu_v_s = [], [], [], [], [], []
    for l in range(DEPTH):
        lam_init = 0.8 - 0.6 * math.exp(-0.3 * l)
        lam = _diff_lambda(diff_lambda[l], lam_init)
        moe_w = (w_rg[l], b_rg[l], w_re[l], b_re[l], w_gu[l], w_down[l])
        proj_w = (w_in[l], b_f[l], g_sgu[l])
        out_w = (g_subln[l], lam_init, g_out_b[l], g_out_c[l], w_out[l])

        def prompt_mixer(h):
            qa, ka, va, ub, vb, qc, kc, vc, lf = _project(h, *proj_w, pos_p)
            oa = _diff_attn_prompt(qa, ka, va, lam)
            ob = _sgu_prompt(ub, vb, w_s[l], b_s[l])
            oc = _fox_prompt(qc, kc, vc, lf)
            return _mix_out(oa, ob, oc, *out_w), (ka, va, kc, vc, lf)

        def sample_mixer(h):
            qa, ka, va, ub, vb, qc, kc, vc, lf = _project(h, *proj_w, pos_s)
            ka_all = jnp.concatenate(
                [cache_a_k[l, page_table].reshape(n_dec, past, H_A, 2, HD), ka], axis=1)
            va_all = jnp.concatenate(
                [cache_a_v[l, page_table].reshape(n_dec, past, H_A, 2 * HD), va], axis=1)
            oa = _diff_core(qa, ka_all, va_all, lam, pos_s, kpos_s)
            ob = _sgu_sample(ub, vb, w_s[l], b_s[l])
            kc_all = jnp.concatenate(
                [cache_c_k[l, page_table].reshape(n_dec, past, H_C, HD), kc], axis=1)
            vc_all = jnp.concatenate(
                [cache_c_v[l, page_table].reshape(n_dec, past, H_C, HD), vc], axis=1)
            d_past = jnp.cumsum(
                cache_c_logf[l, page_table].reshape(n_dec, past, H_C).astype(jnp.float32), axis=1)
            d_new = d_past[:, -1:] + jnp.cumsum(lf, axis=1)
            oc = _fox_core(qc, kc_all, vc_all, d_new, jnp.concatenate([d_past, d_new], axis=1),
                           pos_s, kpos_s)
            return _mix_out(oa, ob, oc, *out_w), (ka, va, kc, vc, lf, vb)

        xp, (ka, va, kc, vc, lf) = _layer(xp, c_prompt, prompt_mixer, w_ada[l], b_ada[l],
                                           g_norm[l], moe_w)
        a_k_p.append(ka.reshape(ka.shape[0], ka.shape[1], H_A, 2 * HD))
        a_v_p.append(va)
        c_k_p.append(kc)
        c_v_p.append(vc)
        c_lf_p.append(lf)
        xs, (ka, va, kc, vc, lf, vb) = _layer(xs, c_sample, sample_mixer, w_ada[l], b_ada[l],
                                               g_norm[l], moe_w)
        a_k_s.append(ka.reshape(n_dec, dec_len, H_A, 2 * HD))
        a_v_s.append(va)
        c_k_s.append(kc)
        c_v_s.append(vc)
        c_lf_s.append(lf)
        sgu_v_s.append(vb)
    return (xp, xs,
            jnp.stack(a_k_p), jnp.stack(a_v_p), jnp.stack(c_k_p), jnp.stack(c_v_p), jnp.stack(c_lf_p),
            jnp.stack(a_k_s), jnp.stack(a_v_s), jnp.stack(c_k_s), jnp.stack(c_v_s), jnp.stack(c_lf_s),
            jnp.stack(sgu_v_s))
```

```python
import functools
import math

import jax
import jax.numpy as jnp
from jax import lax
from jax.experimental import pallas as pl
from jax.experimental.pallas import tpu as pltpu

F32 = jnp.float32
BF16 = jnp.bfloat16
I32 = jnp.int32

D = 2048
HD = 64
H_A = 6
G_B = 8
DB_W = 512
CHUNK = 128
H_C = 12
SEG = 768
N_SEG = 6
OFF_B = 3 * SEG
OFF_C = OFF_B + 2 * DB_W
OFF_F = OFF_C + 3 * SEG
ROT_DIM = 16
ROPE_THETA = 500000.0
N_GROUPS = 4
N_EXP_PER_GROUP = 4
N_EXP = 16
D_EXP = 512
EPS = 1e-6
LANE = 128
NEG = -1e30
VMEM_LIMIT = 56 * 1024 * 1024

TM_IP = 512
TQ = 512
TM_OP = 256
TM_E = 256
TM_CB = 256
DMA_WINDOW = 16
TN_S = 512


def _cparams(*sem):
    return pltpu.CompilerParams(dimension_semantics=sem, vmem_limit_bytes=VMEM_LIMIT)


def _dot(a, b):
    return jnp.dot(a, b, preferred_element_type=F32)


def _dot_nt(a, b):
    return lax.dot_general(a, b, (((1,), (1,)), ((), ())), preferred_element_type=F32)


def _split2(x):
    hi = x.astype(BF16)
    lo = (x - hi.astype(F32)).astype(BF16)
    return hi, lo


def _dot3(x, w):
    m = x.shape[0]
    xh, xl = _split2(x)
    wh, wl = _split2(w)
    y = _dot(jnp.concatenate([xh, xl], axis=0), wh)
    return y[:m] + y[m:] + _dot(xh, wl)


def _dot_exact_lhs(a_bf16, x):
    n = x.shape[1]
    h1 = x.astype(BF16)
    r1 = x - h1.astype(F32)
    h2 = r1.astype(BF16)
    h3 = (r1 - h2.astype(F32)).astype(BF16)
    y = _dot(a_bf16, jnp.concatenate([h1, h2, h3], axis=1))
    return y[:, :n] + y[:, n:2 * n] + y[:, 2 * n:]


def _rms(x, g):
    return x * lax.rsqrt(jnp.mean(x * x, axis=-1, keepdims=True) + EPS) * g


def _prenorm(x, g, sc, sh):
    return _rms(x, g) * (1.0 + sc) + sh


def _silu(x):
    return x * (1.0 / (1.0 + jnp.exp(-x)))


def _gelu(x):
    return x * (0.5 * (1.0 + jnp.tanh(0.7978845608028654 * (x + 0.044715 * (x * x * x)))))


def _log_sigmoid(x):
    return jnp.minimum(x, 0.0) - jnp.log1p(jnp.exp(-jnp.abs(x)))


def _rope(z, c, sa, sb):
    outs = []
    for k in range(z.shape[1] // LANE):
        zz = z[:, k * LANE:(k + 1) * LANE]
        outs.append(zz * c + pltpu.roll(zz, LANE - ROT_DIM // 2, 1) * sa
                    + pltpu.roll(zz, ROT_DIM // 2, 1) * sb)
    return jnp.concatenate(outs, axis=1)


def _route(logits):
    lane = lax.broadcasted_iota(I32, logits.shape, 1).astype(F32)
    gl = jnp.where(lane < N_GROUPS, logits, NEG)
    gmax = jnp.max(gl, axis=1, keepdims=True)
    grp = jnp.min(jnp.where(gl == gmax, lane, float(LANE)), axis=1, keepdims=True)
    p_grp = 1.0 / jnp.sum(jnp.exp(gl - gmax), axis=1, keepdims=True)
    lo = N_GROUPS + N_EXP_PER_GROUP * grp
    el = jnp.where((lane >= lo) & (lane < lo + N_EXP_PER_GROUP), logits, NEG)
    v1 = jnp.max(el, axis=1, keepdims=True)
    i1 = jnp.min(jnp.where(el == v1, lane, float(LANE)), axis=1, keepdims=True)
    el2 = jnp.where(lane == i1, NEG, el)
    v2 = jnp.max(el2, axis=1, keepdims=True)
    i2 = jnp.min(jnp.where(el2 == v2, lane, float(LANE)), axis=1, keepdims=True)
    t = jnp.exp(v2 - v1)
    w1 = p_grp * (1.0 / (1.0 + t))
    w2 = p_grp * (t / (1.0 + t))
    return i1 - N_GROUPS, i2 - N_GROUPS, w1, w2


def _mod_kernel(c_ref, w_ref, b_ref, o_ref):
    o_ref[0] = _dot3(_silu(c_ref[...]), w_ref[0]) + b_ref[0]


def _mod_call(c_all, w_ada, b_ada):
    depth, _, n6 = w_ada.shape
    rows = c_all.shape[0]
    tn = 1024
    return pl.pallas_call(
        _mod_kernel,
        grid=(depth, n6 // tn),
        in_specs=[pl.BlockSpec((rows, D), lambda l, j: (0, 0)),
                  pl.BlockSpec((1, D, tn), lambda l, j: (l, 0, j)),
                  pl.BlockSpec((1, 1, tn), lambda l, j: (l, 0, j))],
        out_specs=pl.BlockSpec((1, rows, tn), lambda l, j: (l, 0, j)),
        out_shape=jax.ShapeDtypeStruct((depth, rows, n6), F32),
        compiler_params=_cparams("parallel", "parallel"),
        name="adaln_mod",
    )(c_all, w_ada, b_ada.reshape(depth, 1, n6))


def _ip1_kernel(x_ref, g_ref, sc_ref, sh_ref, w_ref, cos_ref, sa_ref, sb_ref,
                qa_ref, kaf_ref, kab_ref, vaf_ref, vab_ref,
                qc_ref, kcf_ref, kcb_ref, vcf_ref, vcb_ref, h_scr):
    j = pl.program_id(1)

    @pl.when(j == 0)
    def _():
        h_scr[...] = _prenorm(x_ref[...], g_ref[...], sc_ref[0], sh_ref[0]).astype(BF16)

    z = _dot(h_scr[...], w_ref[...])

    @pl.when(j == 0)
    def _():
        qa_ref[...] = (_rope(z, cos_ref[...], sa_ref[...], sb_ref[...]) * HD ** -0.5).astype(BF16)

    @pl.when(j == 1)
    def _():
        k = _rope(z, cos_ref[...], sa_ref[...], sb_ref[...])
        kaf_ref[...] = k
        kab_ref[...] = k.astype(BF16)

    @pl.when(j == 2)
    def _():
        vaf_ref[...] = z
        vab_ref[...] = z.astype(BF16)

    @pl.when(j == 3)
    def _():
        qc_ref[...] = (z * HD ** -0.5).astype(BF16)

    @pl.when(j == 4)
    def _():
        kcf_ref[...] = z
        kcb_ref[...] = z.astype(BF16)

    @pl.when(j == 5)
    def _():
        vcf_ref[...] = z
        vcb_ref[...] = z.astype(BF16)


def _ip1_call(x, g0, sc, sh, w768, rope_tabs, seq):
    n = x.shape[0]
    tm = min(TM_IP, seq)
    tpb = seq // tm
    bmap = lambda i, j: (i // tpb, 0, 0)
    row = pl.BlockSpec((tm, SEG), lambda i, j: (i, 0))
    tab = pl.BlockSpec((tm, LANE), lambda i, j: (i % tpb, 0))
    f = jax.ShapeDtypeStruct((n, SEG), F32)
    b = jax.ShapeDtypeStruct((n, SEG), BF16)
    return pl.pallas_call(
        _ip1_kernel,
        grid=(n // tm, N_SEG),
        in_specs=[pl.BlockSpec((tm, D), lambda i, j: (i, 0)),
                  pl.BlockSpec((1, D), lambda i, j: (0, 0)),
                  pl.BlockSpec((1, 1, D), bmap), pl.BlockSpec((1, 1, D), bmap),
                  pl.BlockSpec((D, SEG), lambda i, j: (0, j)),
                  tab, tab, tab],
        out_specs=[row] * 10,
        out_shape=[b, f, b, f, b, b, f, b, f, b],
        scratch_shapes=[pltpu.VMEM((tm, D), BF16)],
        compiler_params=_cparams("parallel", "arbitrary"),
        name="prompt_in_proj",
    )(x, g0, sc, sh, w768, *rope_tabs)


def _ip2_kernel(x_ref, g_ref, sc_ref, sh_ref, w_ref, bf_ref, gs_ref,
                ub_ref, vb_ref, lf_ref, d_ref, carry, *, tpb):
    i = pl.program_id(0)
    tm = x_ref.shape[0]
    h = _prenorm(x_ref[...], g_ref[...], sc_ref[0], sh_ref[0]).astype(BF16)
    z = _dot(h, w_ref[...])
    ub_ref[...] = _gelu(z[:, :DB_W])
    vb_ref[...] = _rms(_gelu(z[:, DB_W:2 * DB_W]), gs_ref[...]).astype(BF16)
    lf = _log_sigmoid(z[:, 2 * DB_W:] + bf_ref[...])
    lf_ref[...] = lf

    @pl.when(i % tpb == 0)
    def _():
        carry[...] = jnp.zeros_like(carry)

    r = lax.broadcasted_iota(I32, (tm, tm), 0)
    c = lax.broadcasted_iota(I32, (tm, tm), 1)
    d = _dot_exact_lhs((r >= c).astype(BF16), lf) + carry[...]
    d_ref[...] = d
    carry[...] = d[tm - 1:tm, :]


def _ip2_call(x, g0, sc, sh, wb, bf, gs, seq):
    n = x.shape[0]
    tm = min(TM_IP, seq)
    tpb = seq // tm
    bmap = lambda i: (i // tpb, 0, 0)
    full = lambda shp: pl.BlockSpec(shp, lambda i: (0,) * len(shp))
    return pl.pallas_call(
        functools.partial(_ip2_kernel, tpb=tpb),
        grid=(n // tm,),
        in_specs=[pl.BlockSpec((tm, D), lambda i: (i, 0)), full((1, D)),
                  pl.BlockSpec((1, 1, D), bmap), pl.BlockSpec((1, 1, D), bmap),
                  full((D, 2 * DB_W + LANE)), full((1, LANE)), full((1, DB_W))],
        out_specs=[pl.BlockSpec((tm, DB_W), lambda i: (i, 0)),
                   pl.BlockSpec((tm, DB_W), lambda i: (i, 0)),
                   pl.BlockSpec((tm, LANE), lambda i: (i, 0)),
                   pl.BlockSpec((tm, LANE), lambda i: (i, 0))],
        out_shape=[jax.ShapeDtypeStruct((n, DB_W), F32), jax.ShapeDtypeStruct((n, DB_W), BF16),
                   jax.ShapeDtypeStruct((n, LANE), F32), jax.ShapeDtypeStruct((n, LANE), F32)],
        scratch_shapes=[pltpu.VMEM((1, LANE), F32)],
        compiler_params=_cparams("arbitrary"),
        name="prompt_sgu_gate_proj",
    )(x, g0, sc, sh, wb, bf, gs)


def _flash_kernel(*refs, mode, tq, lam_init):
    if mode == "a":
        q_ref, k_ref, v_ref, dl_ref, gs_ref, o_ref, qq, m_s, l_s, acc = refs
    else:
        q_ref, k_ref, v_ref, dq_ref, dk_ref, o_ref, qq, m_s, l_s, acc = refs
    qi = pl.program_id(2)
    q = q_ref[...]
    lane = lax.broadcasted_iota(I32, q.shape, 1)
    qq[0:tq, :] = jnp.where(lane < HD, q, jnp.zeros_like(q))
    qq[tq:2 * tq, :] = jnp.where(lane >= HD, q, jnp.zeros_like(q))
    m_s[...] = jnp.full_like(m_s, NEG)
    l_s[...] = jnp.zeros_like(l_s)
    acc[...] = jnp.zeros_like(acc)

    def step(kb, diagonal):
        k = k_ref[pl.ds(pl.multiple_of(kb * tq, tq), tq), :]
        v = v_ref[pl.ds(pl.multiple_of(kb * tq, tq), tq), :]
        s = _dot_nt(qq[...], k)
        if mode == "c":
            dq = dq_ref[0, 0]
            dk = dk_ref[0, 0, kb]
            s = s + jnp.concatenate([dq[:, 0:1] - dk[0:1, :], dq[:, 1:2] - dk[1:2, :]], axis=0)
        if diagonal:
            r = lax.broadcasted_iota(I32, (tq, tq), 0)
            c = lax.broadcasted_iota(I32, (tq, tq), 1)
            keep = jnp.concatenate([c <= r, c <= r], axis=0)
            s = jnp.where(keep, s, NEG)
        m_old = m_s[...]
        m_new = jnp.maximum(m_old, jnp.max(s, axis=1, keepdims=True))
        alpha = jnp.exp(m_old - m_new)
        p = jnp.exp(s - m_new)
        l_s[...] = alpha * l_s[...] + jnp.sum(p, axis=1, keepdims=True)
        acc[...] = alpha * acc[...] + _dot(p.astype(BF16), v)
        m_s[...] = m_new

    def body(kb, carry):
        step(kb, False)
        return carry

    lax.fori_loop(0, qi, body, 0)
    step(qi, True)

    o = acc[...] * (1.0 / l_s[...])
    if mode == "a":
        dl = dl_ref[...]
        lam = (jnp.exp(jnp.sum(dl[0:1] * dl[1:2], axis=1, keepdims=True))
               - jnp.exp(jnp.sum(dl[2:3] * dl[3:4], axis=1, keepdims=True)) + lam_init)
        oa = o[0:tq] - lam * o[tq:2 * tq]
        o_ref[...] = (_rms(oa, gs_ref[...]) * (1.0 - lam_init)).astype(o_ref.dtype)
    else:
        o_ref[...] = jnp.where(lane < HD, o[0:tq], o[tq:2 * tq])


def _flash_call(mode, q, k, v, extra, batch, seq, lam_init=0.0):
    n = q.shape[0]
    tq = min(TQ, seq)
    nq = seq // tq
    qspec = pl.BlockSpec((tq, LANE), lambda b, g, i: (b * nq + i, g))
    kvspec = pl.BlockSpec((seq, LANE), lambda b, g, i: (b, g))
    if mode == "a":
        dl, gs = extra
        especs = [pl.BlockSpec((4, HD), lambda b, g, i: (0, 0)),
                  pl.BlockSpec((1, LANE), lambda b, g, i: (0, 0))]
        out_dtype = BF16
    else:
        dq, dk = extra
        especs = [pl.BlockSpec((1, 1, tq, 2), lambda b, g, i: (b, g, i, 0)),
                  pl.BlockSpec((1, 1, nq, 2, tq), lambda b, g, i: (b, g, 0, 0, 0))]
        out_dtype = F32
    return pl.pallas_call(
        functools.partial(_flash_kernel, mode=mode, tq=tq, lam_init=lam_init),
        grid=(batch, SEG // LANE, nq),
        in_specs=[qspec, kvspec, kvspec] + especs,
        out_specs=qspec,
        out_shape=jax.ShapeDtypeStruct((n, SEG), out_dtype),
        scratch_shapes=[pltpu.VMEM((2 * tq, LANE), BF16), pltpu.VMEM((2 * tq, 1), F32),
                        pltpu.VMEM((2 * tq, 1), F32), pltpu.VMEM((2 * tq, LANE), F32)],
        compiler_params=_cparams("parallel", "parallel", "parallel"),
        name="prompt_flash_" + mode,
    )(q, k, v, *extra)


def _op_kernel(oa_ref, oc_ref, ub_ref, vb_ref, ws_ref, bst_ref, gb_ref, gc_ref, wout_ref,
               x_ref, g1_ref, ga1_ref, g2_ref, sc2_ref, sh2_ref, wr_ref, br_ref,
               x1_ref, h2_ref, eid_ref, ew_ref, cat):
    tm = x_ref.shape[0]
    r = lax.broadcasted_iota(I32, (CHUNK, CHUNK), 0)
    c = lax.broadcasted_iota(I32, (CHUNK, CHUNK), 1)
    lane_b = lax.broadcasted_iota(I32, (CHUNK, DB_W), 1)
    w_tril = [jnp.where(r >= c, ws_ref[g], 0.0).astype(BF16) for g in range(G_B)]
    for ch in range(tm // CHUNK):
        rows = slice(ch * CHUNK, (ch + 1) * CHUNK)
        vb = vb_ref[rows, :]
        mix = bst_ref[...]
        for g in range(G_B):
            in_group = (lane_b >= g * HD) & (lane_b < (g + 1) * HD)
            mix = mix + jnp.where(in_group, _dot(w_tril[g], vb), 0.0)
        cat[rows, SEG:SEG + DB_W] = _rms(ub_ref[rows, :] * mix, gb_ref[...]).astype(BF16)
    cat[:, 0:SEG] = oa_ref[...]
    cat[:, SEG + DB_W:] = _rms(oc_ref[...], gc_ref[...]).astype(BF16)
    m = _dot(cat[...], wout_ref[...])
    x1 = x_ref[...] + ga1_ref[0] * _rms(m, g1_ref[...])
    x1_ref[...] = x1
    h2 = _prenorm(x1, g2_ref[...], sc2_ref[0], sh2_ref[0])
    h2_ref[...] = h2
    e1, e2, w1, w2 = _route(_dot(h2.astype(BF16), wr_ref[...]) + br_ref[...])
    lane = lax.broadcasted_iota(I32, (tm, LANE), 1)
    eid_ref[...] = jnp.where(lane == 0, e1, jnp.where(lane == 1, e2, 0.0)).astype(I32)
    ew_ref[...] = jnp.where(lane == 0, w1, jnp.where(lane == 1, w2, 0.0))


def _op_call(oa, oc, ub, vb, ws, bst, gb, gc, wout, x, g1, ga1, g2, sc2, sh2, wr, br, seq):
    n = x.shape[0]
    tm = min(TM_OP, seq)
    tpb = seq // tm
    bmap = lambda i: (i // tpb, 0, 0)
    full = lambda shp: pl.BlockSpec(shp, lambda i: (0,) * len(shp))
    rows = lambda w: pl.BlockSpec((tm, w), lambda i: (i, 0))
    return pl.pallas_call(
        _op_kernel,
        grid=(n // tm,),
        in_specs=[rows(SEG), rows(SEG), rows(DB_W), rows(DB_W),
                  full((G_B, CHUNK, CHUNK)), full((CHUNK, DB_W)), full((1, DB_W)), full((1, SEG)),
                  full((D, D)), rows(D), full((1, D)), pl.BlockSpec((1, 1, D), bmap),
                  full((1, D)), pl.BlockSpec((1, 1, D), bmap), pl.BlockSpec((1, 1, D), bmap),
                  full((D, LANE)), full((1, LANE))],
        out_specs=[rows(D), rows(D), rows(LANE), rows(LANE)],
        out_shape=[jax.ShapeDtypeStruct((n, D), F32), jax.ShapeDtypeStruct((n, D), F32),
                   jax.ShapeDtypeStruct((n, LANE), I32), jax.ShapeDtypeStruct((n, LANE), F32)],
        scratch_shapes=[pltpu.VMEM((tm, D), BF16)],
        compiler_params=_cparams("parallel"),
        name="prompt_mix_out_router",
    )(oa, oc, ub, vb, ws, bst, gb, gc, wout, x, g1, ga1, g2, sc2, sh2, wr, br)


def _gather_kernel(tok_ref, src_ref, dst_ref, sem):
    total = dst_ref.shape[0]

    def copy(slot):
        return pltpu.make_async_copy(src_ref.at[pl.ds(tok_ref[slot], 1)],
                                     dst_ref.at[pl.ds(slot, 1)], sem)

    def issue(slot, carry):
        copy(slot).start()

        @pl.when(slot >= DMA_WINDOW)
        def _():
            copy(slot - DMA_WINDOW).wait()

        return carry

    def drain(slot, carry):
        copy(slot).wait()
        return carry

    lax.fori_loop(0, total, issue, 0)
    lax.fori_loop(total - DMA_WINDOW, total, drain, 0)


def _gather_call(tok_of_slot, src):
    total = tok_of_slot.shape[0]
    return pl.pallas_call(
        _gather_kernel,
        grid_spec=pltpu.PrefetchScalarGridSpec(
            num_scalar_prefetch=1, grid=(1,),
            in_specs=[pl.BlockSpec(memory_space=pl.ANY)],
            out_specs=pl.BlockSpec(memory_space=pl.ANY),
            scratch_shapes=[pltpu.SemaphoreType.DMA(())]),
        out_shape=jax.ShapeDtypeStruct((total, src.shape[1]), src.dtype),
        compiler_params=_cparams("arbitrary"),
        name="moe_row_gather",
    )(tok_of_slot, src)


def _moe_kernel(te_ref, nv_ref, xs_ref, gw_ref, wgu_ref, wd_ref, y_ref, wgu_b, wd_b):
    t = pl.program_id(0)
    prev = te_ref[jnp.maximum(t - 1, 0)]
    valid = t < nv_ref[0]

    @pl.when(valid & ((t == 0) | (te_ref[t] != prev)))
    def _():
        wgu_b[...] = wgu_ref[0, 0].astype(BF16)
        wd_b[...] = wd_ref[0, 0].astype(BF16)

    @pl.when(valid)
    def _():
        hu = _dot(xs_ref[...].astype(BF16), wgu_b[...])
        hid = _silu(hu[:, :D_EXP]) * hu[:, D_EXP:] * gw_ref[...]
        y_ref[...] = _dot(hid.astype(BF16), wd_b[...])

    @pl.when(jnp.logical_not(valid))
    def _():
        y_ref[...] = jnp.zeros_like(y_ref)


def _moe_call(layer, tile_e, n_valid, xs, gw, w_gu, w_down):
    total = xs.shape[0]
    n_tiles = total // TM_E
    clamp = lambda t, te, nv: (jnp.minimum(t, nv[0] - 1), 0)
    return pl.pallas_call(
        _moe_kernel,
        grid_spec=pltpu.PrefetchScalarGridSpec(
            num_scalar_prefetch=2, grid=(n_tiles,),
            in_specs=[pl.BlockSpec((TM_E, D), clamp),
                      pl.BlockSpec((TM_E, 1), clamp),
                      pl.BlockSpec((1, 1, D, 2 * D_EXP), lambda t, te, nv: (layer, te[t], 0, 0)),
                      pl.BlockSpec((1, 1, D_EXP, D), lambda t, te, nv: (layer, te[t], 0, 0))],
            out_specs=pl.BlockSpec((TM_E, D), lambda t, te, nv: (t, 0)),
            scratch_shapes=[pltpu.VMEM((D, 2 * D_EXP), BF16), pltpu.VMEM((D_EXP, D), BF16)]),
        out_shape=jax.ShapeDtypeStruct((total, D), F32),
        compiler_params=_cparams("arbitrary"),
        name="moe_experts",
    )(tile_e, n_valid, xs, gw, w_gu, w_down)


def _combine_kernel(pos_ref, y_ref, x1_ref, ga_ref, g3_ref, x2_ref, buf, sem, *, n_tok):
    i = pl.program_id(0)
    tm = x1_ref.shape[0]

    def copy(k, row):
        slot = pos_ref[k * n_tok + i * tm + row]
        return pltpu.make_async_copy(y_ref.at[pl.ds(slot, 1)], buf.at[k, pl.ds(row, 1)], sem)

    def issue(row, carry):
        copy(0, row).start()
        copy(1, row).start()
        return carry

    def drain(row, carry):
        copy(0, row).wait()
        copy(1, row).wait()
        return carry

    lax.fori_loop(0, tm, issue, 0)
    lax.fori_loop(0, tm, drain, 0)
    x2_ref[...] = x1_ref[...] + ga_ref[0] * _rms(buf[0] + buf[1], g3_ref[...])


def _combine_call(pos, y, x1, ga2, g3, seq):
    n = x1.shape[0]
    tm = min(TM_CB, seq)
    tpb = seq // tm
    return pl.pallas_call(
        functools.partial(_combine_kernel, n_tok=n),
        grid_spec=pltpu.PrefetchScalarGridSpec(
            num_scalar_prefetch=1, grid=(n // tm,),
            in_specs=[pl.BlockSpec(memory_space=pl.ANY),
                      pl.BlockSpec((tm, D), lambda i, pos: (i, 0)),
                      pl.BlockSpec((1, 1, D), lambda i, pos: (i // tpb, 0, 0)),
                      pl.BlockSpec((1, D), lambda i, pos: (0, 0))],
            out_specs=pl.BlockSpec((tm, D), lambda i, pos: (i, 0)),
            scratch_shapes=[pltpu.VMEM((2, tm, D), F32), pltpu.SemaphoreType.DMA(())]),
        out_shape=jax.ShapeDtypeStruct((n, D), F32),
        compiler_params=_cparams("arbitrary"),
        name="moe_combine",
    )(pos, y, x1, ga2, g3)


def _moe_plan(eid, ew):
    n = eid.shape[0]
    pe = jnp.concatenate([eid[:, 0], eid[:, 1]])
    pw = jnp.concatenate([ew[:, 0], ew[:, 1]])
    tok = jnp.concatenate([jnp.arange(n, dtype=I32)] * 2)
    onehot = (pe[:, None] == jnp.arange(N_EXP, dtype=I32)[None, :]).astype(I32)
    counts = jnp.sum(onehot, axis=0)
    rank = jnp.sum((jnp.cumsum(onehot, axis=0) - onehot) * onehot, axis=1)
    padded = ((counts + TM_E - 1) // TM_E) * TM_E
    ends = jnp.cumsum(padded)
    dest = (ends - padded)[pe] + rank
    total = 2 * n + N_EXP * TM_E
    tok_of_slot = jnp.zeros((total,), I32).at[dest].set(tok)
    gw = jnp.zeros((total,), F32).at[dest].set(pw)
    n_valid = (ends[-1] // TM_E).astype(I32)
    starts = jnp.arange(total // TM_E, dtype=I32) * TM_E
    tile_e = jnp.minimum(jnp.sum((starts[:, None] >= ends[None, :]).astype(I32), axis=1), N_EXP - 1)
    tile_e = jnp.where(starts // TM_E < n_valid, tile_e, tile_e[jnp.maximum(n_valid - 1, 0)])
    return tok_of_slot, gw.reshape(total, 1), tile_e.astype(I32), n_valid.reshape(1), dest.astype(I32)


def _sk1_kernel(x_ref, g_ref, sc_ref, sh_ref, w_ref, wf_ref, z_ref, zf_ref, h_scr):
    j = pl.program_id(0)

    @pl.when(j == 0)
    def _():
        h = _prenorm(x_ref[...], g_ref[...], sc_ref[...], sh_ref[...])
        h_scr[...] = h
        zf_ref[...] = _dot3(h, wf_ref[...])

    z_ref[...] = _dot3(h_scr[...], w_ref[0])


def _sk1_call(layer, x, g0, sc, sh, w_in, wf):
    nb = x.shape[0]
    full = lambda shp: pl.BlockSpec(shp, lambda j: (0,) * len(shp))
    return pl.pallas_call(
        _sk1_kernel,
        grid=(OFF_F // TN_S,),
        in_specs=[full((nb, D)), full((1, D)), full((nb, D)), full((nb, D)),
                  pl.BlockSpec((1, D, TN_S), lambda j: (layer, 0, j)), full((D, LANE))],
        out_specs=[pl.BlockSpec((nb, TN_S), lambda j: (0, j)), full((nb, LANE))],
        out_shape=[jax.ShapeDtypeStruct((nb, OFF_F), F32), jax.ShapeDtypeStruct((nb, LANE), F32)],
        scratch_shapes=[pltpu.VMEM((nb, D), F32)],
        compiler_params=_cparams("arbitrary"),
        name="decode_in_proj",
    )(x, g0, sc, sh, w_in, wf)


def _sk2_kernel(z_ref, zf_ref, cos_ref, sa_ref, sb_ref, bf_ref, gs_ref,
                qa_ref, ka_ref, va_ref, ub_ref, vb_ref, qc_ref, kc_ref, vc_ref, lf_ref):
    c, sa, sb = cos_ref[...], sa_ref[...], sb_ref[...]
    qa_ref[...] = _rope(z_ref[:, 0:SEG], c, sa, sb) * HD ** -0.5
    ka_ref[...] = _rope(z_ref[:, SEG:2 * SEG], c, sa, sb)
    va_ref[...] = z_ref[:, 2 * SEG:OFF_B]
    ub_ref[...] = _gelu(z_ref[:, OFF_B:OFF_B + DB_W])
    vb_ref[...] = _rms(_gelu(z_ref[:, OFF_B + DB_W:OFF_C]), gs_ref[...])
    qc_ref[...] = z_ref[:, OFF_C:OFF_C + SEG] * HD ** -0.5
    kc_ref[...] = z_ref[:, OFF_C + SEG:OFF_C + 2 * SEG]
    vc_ref[...] = z_ref[:, OFF_C + 2 * SEG:OFF_F]
    lf_ref[...] = _log_sigmoid(zf_ref[...] + bf_ref[...])


def _sk2_call(z, zf, rope_tabs, bf, gs):
    nb = z.shape[0]
    s = lambda w: jax.ShapeDtypeStruct((nb, w), F32)
    return pl.pallas_call(
        _sk2_kernel,
        out_shape=[s(SEG), s(SEG), s(SEG), s(DB_W), s(DB_W), s(SEG), s(SEG), s(SEG), s(LANE)],
        name="decode_proj_post",
    )(z, zf, *rope_tabs, bf, gs)


def _lanes_sum_exact(acc):
    ones = jnp.ones((8, acc.shape[1]), BF16)
    h1 = acc.astype(BF16)
    r1 = acc - h1.astype(F32)
    h2 = r1.astype(BF16)
    h3 = (r1 - h2.astype(F32)).astype(BF16)
    y = _dot_nt(ones, h1) + _dot_nt(ones, h2) + _dot_nt(ones, h3)
    return y[0:1]


def _sk3_kernel(pt_ref, qa_ref, kan_ref, van_ref, qc_ref, kcn_ref, qct_ref, vct_ref, lfn_ref, dl_ref,
                ak_ref, av_ref, ck_ref, cv_ref, clf_ref, oa_ref, oc_ref,
                qa_s, m_a, l_a, acc_a, qc_s, m_c, l_c, acc_c, carry, *, lam_init):
    p = pl.program_id(1)
    rows = lax.broadcasted_iota(I32, (8, 2 * HD), 0)
    lanes = lax.broadcasted_iota(I32, (8, 2 * HD), 1)
    map_mask = ((rows == 0) & (lanes < HD)) | ((rows == 1) & (lanes >= HD))
    lane1 = lax.broadcasted_iota(I32, (1, CHUNK), 1)

    @pl.when(p == 0)
    def _():
        for h in range(H_A):
            q2 = jnp.where(map_mask, jnp.broadcast_to(qa_ref[0, h:h + 1, :], (8, 2 * HD)), 0.0)
            qa_s[h] = q2
            m_a[h] = jnp.sum(q2 * kan_ref[0, h:h + 1, :], axis=1, keepdims=True)
            acc_a[h] = jnp.broadcast_to(van_ref[0, h:h + 1, :], (8, 2 * HD))
        l_a[...] = jnp.ones_like(l_a)
        m_c[...] = jnp.sum(qc_ref[0] * kcn_ref[0], axis=1, keepdims=True)
        l_c[...] = jnp.ones_like(l_c)
        for h in range(H_C):
            qc_s[h] = jnp.broadcast_to(qct_ref[0, :, h:h + 1], (HD, CHUNK))
            acc_c[h] = jnp.where(lane1 == 0, jnp.broadcast_to(vct_ref[0, :, h:h + 1], (HD, CHUNK)), 0.0)
        r16 = lax.broadcasted_iota(I32, (16, LANE), 0)
        l16 = lax.broadcasted_iota(I32, (16, LANE), 1)
        carry[...] = jnp.sum(jnp.where(r16 == l16, jnp.broadcast_to(lfn_ref[0], (16, LANE)), 0.0),
                             axis=1, keepdims=True)

    for h in range(H_A):
        s = _dot_nt(qa_s[h].astype(BF16), ak_ref[0, 0, h].astype(BF16))
        m_old = m_a[h]
        m_new = jnp.maximum(m_old, jnp.max(s, axis=1, keepdims=True))
        alpha = jnp.exp(m_old - m_new)
        pr = jnp.exp(s - m_new)
        l_a[h] = alpha * l_a[h] + jnp.sum(pr, axis=1, keepdims=True)
        acc_a[h] = alpha * acc_a[h] + _dot(pr.astype(BF16), av_ref[0, 0, h].astype(BF16))
        m_a[h] = m_new

    lf = clf_ref[0, 0]
    i_ = lax.broadcasted_iota(I32, (CHUNK, CHUNK), 0)
    j_ = lax.broadcasted_iota(I32, (CHUNK, CHUNK), 1)
    h1 = lf.astype(BF16)
    r1 = lf - h1.astype(F32)
    h2 = r1.astype(BF16)
    h3 = (r1 - h2.astype(F32)).astype(BF16)
    y = _dot(jnp.concatenate([h1, h2, h3], axis=0), (i_ > j_).astype(BF16))
    bias = carry[...] + (y[0:16] + y[16:32] + y[32:48])
    for h in range(H_C):
        s = jnp.sum(ck_ref[0, 0, h] * qc_s[h], axis=0, keepdims=True) + bias[h:h + 1, :]
        m_old = m_c[h:h + 1, :]
        m_new = jnp.maximum(m_old, jnp.max(s, axis=1, keepdims=True))
        alpha = jnp.exp(m_old - m_new)
        pr = jnp.exp(s - m_new)
        l_c[h:h + 1, :] = alpha * l_c[h:h + 1, :] + jnp.sum(pr, axis=1, keepdims=True)
        acc_c[h] = alpha * acc_c[h] + cv_ref[0, 0, h] * pr
        m_c[h:h + 1, :] = m_new
    carry[...] = carry[...] + jnp.sum(lf, axis=1, keepdims=True)

    @pl.when(p == pl.num_programs(1) - 1)
    def _():
        dl = dl_ref[...]
        lam = (jnp.exp(jnp.sum(dl[0:1] * dl[1:2], axis=1, keepdims=True))
               - jnp.exp(jnp.sum(dl[2:3] * dl[3:4], axis=1, keepdims=True)) + lam_init)
        for h in range(H_A):
            o = acc_a[h] * (1.0 / l_a[h])
            oa_ref[0, h:h + 1, :] = o[0:1] - lam * o[1:2]
        for h in range(H_C):
            oc_ref[0, h:h + 1, :] = _lanes_sum_exact(acc_c[h]) * (1.0 / l_c[h:h + 1, :])


def _sk3_call(layer, page_table, qa, ka, va, qc, kc, vc, lf, dl, cak, cav, cck, ccv, clf_t, lam_init):
    nb, n_pages = page_table.shape
    heads = lambda a, h: a.reshape(nb, h, a.shape[1] // h)
    pad16 = lambda a: jnp.pad(a, ((0, 0), (0, 16 - a.shape[1]), (0, 0)))
    qc3, kc3, vc3 = pad16(heads(qc, H_C)), pad16(heads(kc, H_C)), pad16(heads(vc, H_C))
    per_b = lambda shp: pl.BlockSpec((1,) + shp, lambda b, p, pt: (b, 0, 0))
    page = lambda b, p, pt: (layer, pt[b * n_pages + n_pages - 1 - p], 0, 0, 0)
    page4 = lambda b, p, pt: (layer, pt[b * n_pages + n_pages - 1 - p], 0, 0)
    return pl.pallas_call(
        functools.partial(_sk3_kernel, lam_init=lam_init),
        grid_spec=pltpu.PrefetchScalarGridSpec(
            num_scalar_prefetch=1, grid=(nb, n_pages),
            in_specs=[per_b((H_A, 2 * HD)), per_b((H_A, 2 * HD)), per_b((H_A, 2 * HD)),
                      per_b((16, HD)), per_b((16, HD)), per_b((HD, 16)), per_b((HD, 16)),
                      per_b((1, LANE)),
                      pl.BlockSpec((4, HD), lambda b, p, pt: (0, 0)),
                      pl.BlockSpec((1, 1, H_A, CHUNK, 2 * HD), page),
                      pl.BlockSpec((1, 1, H_A, CHUNK, 2 * HD), page),
                      pl.BlockSpec((1, 1, H_C, HD, CHUNK), page),
                      pl.BlockSpec((1, 1, H_C, HD, CHUNK), page),
                      pl.BlockSpec((1, 1, 16, CHUNK), page4)],
            out_specs=[per_b((H_A, 2 * HD)), per_b((H_C, HD))],
            scratch_shapes=[pltpu.VMEM((H_A, 8, 2 * HD), F32), pltpu.VMEM((H_A, 8, 1), F32),
                            pltpu.VMEM((H_A, 8, 1), F32), pltpu.VMEM((H_A, 8, 2 * HD), F32),
                            pltpu.VMEM((H_C, HD, CHUNK), F32), pltpu.VMEM((16, 1), F32),
                            pltpu.VMEM((16, 1), F32), pltpu.VMEM((H_C, HD, CHUNK), F32),
                            pltpu.VMEM((16, 1), F32)]),
        out_shape=[jax.ShapeDtypeStruct((nb, H_A, 2 * HD), F32), jax.ShapeDtypeStruct((nb, H_C, HD), F32)],
        compiler_params=_cparams("parallel", "arbitrary"),
        name="decode_paged_attention",
    )(page_table.reshape(-1), heads(qa, H_A), heads(ka, H_A), heads(va, H_A),
      qc3, kc3, jnp.swapaxes(qc3, 1, 2), jnp.swapaxes(vc3, 1, 2), lf.reshape(nb, 1, LANE), dl,
      cak, cav, cck, ccv, clf_t)


def _sk4_kernel(oa_ref, oc_ref, ub_ref, vb_ref, ws0_ref, bs0_ref, gsub_ref, gb_ref, gc_ref, w_ref,
                m_ref, cat, *, lam_init):
    j = pl.program_id(0)

    @pl.when(j == 0)
    def _():
        oa = oa_ref[...]
        parts = []
        for h in range(H_A):
            parts.append(_rms(oa[:, h * 2 * HD:(h + 1) * 2 * HD], gsub_ref[...]) * (1.0 - lam_init))
        parts.append(_rms(ub_ref[...] * (ws0_ref[...] * vb_ref[...] + bs0_ref[...]), gb_ref[...]))
        parts.append(_rms(oc_ref[...], gc_ref[...]))
        cat[...] = jnp.concatenate(parts, axis=1)

    m_ref[...] = _dot3(cat[...], w_ref[0])


def _sk4_call(layer, oa, oc, ub, vb, ws0, bs0, gsub, gb, gc, w_out, lam_init):
    nb = oa.shape[0]
    full = lambda shp: pl.BlockSpec(shp, lambda j: (0,) * len(shp))
    return pl.pallas_call(
        functools.partial(_sk4_kernel, lam_init=lam_init),
        grid=(D // TN_S,),
        in_specs=[full((nb, SEG)), full((nb, SEG)), full((nb, DB_W)), full((nb, DB_W)),
                  full((1, DB_W)), full((1, DB_W)), full((1, 2 * HD)), full((1, DB_W)), full((1, SEG)),
                  pl.BlockSpec((1, D, TN_S), lambda j: (layer, 0, j))],
        out_specs=pl.BlockSpec((nb, TN_S), lambda j: (0, j)),
        out_shape=jax.ShapeDtypeStruct((nb, D), F32),
        scratch_shapes=[pltpu.VMEM((nb, D), F32)],
        compiler_params=_cparams("arbitrary"),
        name="decode_out_proj",
    )(oa, oc, ub, vb, ws0, bs0, gsub, gb, gc, w_out)


def _sk4b_kernel(m_ref, x_ref, g1_ref, ga1_ref, g2_ref, sc2_ref, sh2_ref, wr_ref, br_ref,
                 x1_ref, h2_ref, gate_ref):
    x1 = x_ref[...] + ga1_ref[...] * _rms(m_ref[...], g1_ref[...])
    x1_ref[...] = x1
    h2 = _prenorm(x1, g2_ref[...], sc2_ref[...], sh2_ref[...])
    h2_ref[...] = h2
    e1, e2, w1, w2 = _route(_dot3(h2, wr_ref[...]) + br_ref[...])
    lane = lax.broadcasted_iota(I32, gate_ref.shape, 1).astype(F32)
    gate_ref[...] = jnp.where(lane == e1, w1, 0.0) + jnp.where(lane == e2, w2, 0.0)


def _sk4b_call(m, x, g1, ga1, g2, sc2, sh2, wr, br):
    nb = x.shape[0]
    return pl.pallas_call(
        _sk4b_kernel,
        out_shape=[jax.ShapeDtypeStruct((nb, D), F32), jax.ShapeDtypeStruct((nb, D), F32),
                   jax.ShapeDtypeStruct((nb, LANE), F32)],
        name="decode_post_router",
    )(m, x, g1, ga1, g2, sc2, sh2, wr, br)


def _sk5_kernel(el_ref, cnt_ref, h2_ref, gate_ref, x1_ref, ga_ref, g3_ref, wgu_ref, wd_ref,
                x2_ref, acc):
    s = pl.program_id(0)

    @pl.when(s == 0)
    def _():
        acc[...] = jnp.zeros_like(acc)

    @pl.when(s < cnt_ref[0])
    def _():
        hu = _dot3(h2_ref[...], wgu_ref[0, 0])
        lane = lax.broadcasted_iota(I32, gate_ref.shape, 1)
        g = jnp.sum(jnp.where(lane == el_ref[s], gate_ref[...], 0.0), axis=1, keepdims=True)
        hid = _silu(hu[:, :D_EXP]) * hu[:, D_EXP:] * g
        acc[...] += _dot3(hid, wd_ref[0, 0])

    @pl.when(s == pl.num_programs(0) - 1)
    def _():
        x2_ref[...] = x1_ref[...] + ga_ref[...] * _rms(acc[...], g3_ref[...])


def _sk5_call(layer, elist, cnt, h2, gate, x1, ga2, g3, w_gu, w_down):
    nb = h2.shape[0]
    full = lambda shp: pl.BlockSpec(shp, lambda s, el, cnt: (0,) * len(shp))
    return pl.pallas_call(
        _sk5_kernel,
        grid_spec=pltpu.PrefetchScalarGridSpec(
            num_scalar_prefetch=2, grid=(N_EXP,),
            in_specs=[full((nb, D)), full((nb, LANE)), full((nb, D)), full((nb, D)), full((1, D)),
                      pl.BlockSpec((1, 1, D, 2 * D_EXP), lambda s, el, cnt: (layer, el[s], 0, 0)),
                      pl.BlockSpec((1, 1, D_EXP, D), lambda s, el, cnt: (layer, el[s], 0, 0))],
            out_specs=full((nb, D)),
            scratch_shapes=[pltpu.VMEM((nb, D), F32)]),
        out_shape=jax.ShapeDtypeStruct((nb, D), F32),
        compiler_params=_cparams("arbitrary"),
        name="decode_moe",
    )(elist, cnt, h2, gate, x1, ga2, g3, w_gu, w_down)


def _rope_tables(pos):
    half = ROT_DIM // 2
    n = pos.shape[0]
    inv = ROPE_THETA ** (-jnp.arange(0, ROT_DIM, 2, dtype=F32) / ROT_DIM)
    ang = pos.astype(F32)[:, None] * inv[None, :]
    cos, sin = jnp.cos(ang), jnp.sin(ang)
    one = jnp.ones((n, HD - ROT_DIM), F32)
    zero = jnp.zeros((n, HD - ROT_DIM), F32)
    z8 = jnp.zeros((n, half), F32)
    c64 = jnp.concatenate([cos, cos, one], axis=1)
    sa64 = jnp.concatenate([-sin, z8, zero], axis=1)
    sb64 = jnp.concatenate([z8, sin, zero], axis=1)
    rep = lambda t: jnp.concatenate([t, t], axis=1)
    return rep(c64), rep(sa64), rep(sb64)


def _pad_lanes(a):
    return jnp.pad(a, ((0, 0), (0, LANE - a.shape[1])))


def kernel(x_prompt, x_sample, c_prompt, c_sample, cache_a_k, cache_a_v, cache_c_k, cache_c_v,
           cache_c_logf, page_table, w_ada, b_ada, g_norm, w_in, b_f, g_sgu, w_s, b_s,
           diff_lambda, g_subln, g_out_b, g_out_c, w_out, w_rg, b_rg, w_re, b_re, w_gu, w_down):
    depth = w_ada.shape[0]
    batch, seq, _ = x_prompt.shape
    nb, dec_len, _ = x_sample.shape
    assert dec_len == 1 and seq % CHUNK == 0
    n_pool, page = cache_a_k.shape[1], cache_a_k.shape[2]
    assert page == CHUNK
    past = page_table.shape[1] * page
    n = batch * seq

    rows_c = 16
    c_all = jnp.zeros((rows_c, D), F32).at[:batch].set(c_prompt).at[batch:batch + nb].set(c_sample)
    mod = _mod_call(c_all, w_ada, b_ada)

    tabs_p = _rope_tables(jnp.arange(seq))
    tabs_s = _rope_tables(jnp.full((1,), past))
    cak = jnp.transpose(cache_a_k, (0, 1, 3, 2, 4))
    cav = jnp.transpose(cache_a_v, (0, 1, 3, 2, 4))
    cck = jnp.transpose(cache_c_k, (0, 1, 3, 4, 2))
    ccv = jnp.transpose(cache_c_v, (0, 1, 3, 4, 2))
    clf_t = jnp.pad(jnp.swapaxes(cache_c_logf, 2, 3), ((0, 0), (0, 0), (0, 16 - H_C), (0, 0)))

    tq = min(TQ, seq)
    nq = seq // tq
    xp = x_prompt.reshape(n, D)
    xs = x_sample.reshape(nb, D)
    outs = [[] for _ in range(11)]
    for l in range(depth):
        lam_init = 0.8 - 0.6 * math.exp(-0.3 * l)
        row = lambda v: v.reshape(1, -1)
        gn = [row(g_norm[l, k]) for k in range(4)]
        mp = [mod[l, :batch, k * D:(k + 1) * D].reshape(batch, 1, D) for k in range(6)]
        ms = [mod[l, batch:batch + nb, k * D:(k + 1) * D] for k in range(6)]
        wl = w_in[l]
        w768 = jnp.concatenate([wl[:, :OFF_B], wl[:, OFF_C:OFF_F]], axis=1).astype(BF16)
        wf = _pad_lanes(wl[:, OFF_F:])
        wb = jnp.concatenate([wl[:, OFF_B:OFF_C], wf], axis=1).astype(BF16)
        bf = _pad_lanes(row(b_f[l]))
        gs = row(g_sgu[l])
        wr = _pad_lanes(jnp.concatenate([w_rg[l], w_re[l]], axis=1))
        br = _pad_lanes(row(jnp.concatenate([b_rg[l], b_re[l]])))
        gsub, gb, gc = row(g_subln[l]), row(g_out_b[l]), row(g_out_c[l])
        dl = diff_lambda[l]

        qa, kaf, kab, vaf, vab, qc, kcf, kcb, vcf, vcb = _ip1_call(xp, gn[0], mp[1], mp[0], w768, tabs_p, seq)
        ub, vb, lfp, dcum = _ip2_call(xp, gn[0], mp[1], mp[0], wb, bf, gs, seq)
        oa = _flash_call("a", qa, kab, vab, (dl, gsub), batch, seq, lam_init)
        dh = dcum[:, :H_C].reshape(batch, nq, tq, H_C // 2, 2)
        dq = jnp.transpose(dh, (0, 3, 1, 2, 4)).reshape(batch, H_C // 2, seq, 2)
        dk = jnp.transpose(dh, (0, 3, 1, 4, 2))
        oc = _flash_call("c", qc, kcb, vcb, (dq, dk), batch, seq)
        bst = jnp.repeat(jnp.swapaxes(b_s[l], 0, 1), HD, axis=1)
        x1, h2, eid, ew = _op_call(oa, oc, ub, vb, w_s[l], bst, gb, gc, w_out[l].astype(BF16),
                                   xp, gn[1], mp[2], gn[2], mp[4], mp[3], wr.astype(BF16), br, seq)
        tok_of_slot, gw, tile_e, n_valid, dest = _moe_plan(eid[:, :2], ew[:, :2])
        xsort = _gather_call(tok_of_slot, h2)
        y = _moe_call(l, tile_e, n_valid, xsort, gw, w_gu, w_down)
        xp = _combine_call(dest, y, x1, mp[5], gn[3], seq)
        for k, a in enumerate((kaf, vaf, kcf, vcf, lfp[:, :H_C])):
            outs[k].append(a)

        z, zf = _sk1_call(l, xs, gn[0], ms[1], ms[0], w_in, wf)
        qa_s, ka_s, va_s, ub_s, vb_s, qc_s, kc_s, vc_s, lf_s = _sk2_call(z, zf, tabs_s, bf, gs)
        oa_s, oc_s = _sk3_call(l, page_table, qa_s, ka_s, va_s, qc_s, kc_s, vc_s, lf_s, dl,
                               cak, cav, cck, ccv, clf_t, lam_init)
        ws0 = row(jnp.repeat(w_s[l, :, 0, 0], HD))
        bs0 = row(jnp.repeat(b_s[l, :, 0], HD))
        m_s = _sk4_call(l, oa_s.reshape(nb, SEG), oc_s.reshape(nb, SEG), ub_s, vb_s, ws0, bs0,
                        gsub, gb, gc, w_out, lam_init)
        x1_s, h2_s, gate = _sk4b_call(m_s, xs, gn[1], ms[2], gn[2], ms[4], ms[3], wr, br)
        used = jnp.any(gate[:, :N_EXP] > 0.0, axis=0)
        order = jnp.argsort(jnp.logical_not(used), stable=True).astype(I32)
        cnt = jnp.sum(used).astype(I32)
        elist = jnp.where(jnp.arange(N_EXP) < cnt, order, order[jnp.maximum(cnt - 1, 0)])
        xs = _sk5_call(l, elist, cnt.reshape(1), h2_s, gate, x1_s, ms[5], gn[3], w_gu, w_down)
        for k, a in enumerate((ka_s, va_s, kc_s, vc_s, lf_s[:, :H_C], vb_s)):
            outs[5 + k].append(a)

    st = lambda k: jnp.stack(outs[k])
    return (xp.reshape(batch, seq, D), xs.reshape(nb, 1, D),
            st(0).reshape(depth, batch, seq, H_A, 2 * HD), st(1).reshape(depth, batch, seq, H_A, 2 * HD),
            st(2).reshape(depth, batch, seq, H_C, HD), st(3).reshape(depth, batch, seq, H_C, HD),
            st(4).reshape(depth, batch, seq, H_C),
            st(5).reshape(depth, nb, 1, H_A, 2 * HD), st(6).reshape(depth, nb, 1, H_A, 2 * HD),
            st(7).reshape(depth, nb, 1, H_C, HD), st(8).reshape(depth, nb, 1, H_C, HD),
            st(9).reshape(depth, nb, 1, H_C), st(10).reshape(depth, nb, 1, G_B, HD))
```

```python
import functools
import math

import jax
import jax.numpy as jnp
from jax import lax
from jax.experimental import pallas as pl
from jax.experimental.pallas import tpu as pltpu

F32 = jnp.float32
BF16 = jnp.bfloat16
I32 = jnp.int32

D = 2048
HD = 64
H_A = 6
G_B = 8
DB_W = 512
CHUNK = 128
H_C = 12
SEG = 768
N_SEG = 6
OFF_B = 3 * SEG
OFF_C = OFF_B + 2 * DB_W
OFF_F = OFF_C + 3 * SEG
ROT_DIM = 16
ROPE_THETA = 500000.0
N_GROUPS = 4
N_EXP_PER_GROUP = 4
N_EXP = 16
D_EXP = 512
EPS = 1e-6
LANE = 128
NEG = -1e30
LOG2E = 1.4426950408889634
VMEM_LIMIT = 56 * 1024 * 1024

TM_IP = 512
TQ = 512
TM_OP = 256
TM_E = 256
TM_CB = 256
TN_S = 512
PAGES_PER_STEP = 4
ROWS_A = 16


def _cparams(*sem):
    return pltpu.CompilerParams(dimension_semantics=sem, vmem_limit_bytes=VMEM_LIMIT)


def _dot(a, b):
    return jnp.dot(a, b, preferred_element_type=F32)


def _dot_nt(a, b):
    return lax.dot_general(a, b, (((1,), (1,)), ((), ())), preferred_element_type=F32)


def _split2(x):
    hi = x.astype(BF16)
    lo = (x - hi.astype(F32)).astype(BF16)
    return hi, lo


def _dot3(x, w):
    m = x.shape[0]
    xh, xl = _split2(x)
    wh, wl = _split2(w)
    y = _dot(jnp.concatenate([xh, xl], axis=0), wh)
    return y[:m] + y[m:] + _dot(xh, wl)


def _dot_exact_lhs(a_bf16, x):
    n = x.shape[1]
    h1 = x.astype(BF16)
    r1 = x - h1.astype(F32)
    h2 = r1.astype(BF16)
    h3 = (r1 - h2.astype(F32)).astype(BF16)
    y = _dot(a_bf16, jnp.concatenate([h1, h2, h3], axis=1))
    return y[:, :n] + y[:, n:2 * n] + y[:, 2 * n:]


def _rms(x, g):
    return x * lax.rsqrt(jnp.mean(x * x, axis=-1, keepdims=True) + EPS) * g


def _prenorm(x, g, sc, sh):
    return _rms(x, g) * (1.0 + sc) + sh


def _silu(x):
    return x * (1.0 / (1.0 + jnp.exp(-x)))


def _gelu(x):
    return x * (0.5 * (1.0 + jnp.tanh(0.7978845608028654 * (x + 0.044715 * (x * x * x)))))


def _log_sigmoid(x):
    return jnp.minimum(x, 0.0) - jnp.log1p(jnp.exp(-jnp.abs(x)))


def _rope(z, c, sa, sb):
    outs = []
    for k in range(z.shape[1] // LANE):
        zz = z[:, k * LANE:(k + 1) * LANE]
        outs.append(zz * c + pltpu.roll(zz, LANE - ROT_DIM // 2, 1) * sa
                    + pltpu.roll(zz, ROT_DIM // 2, 1) * sb)
    return jnp.concatenate(outs, axis=1)


def _route(logits):
    lane = lax.broadcasted_iota(I32, logits.shape, 1).astype(F32)
    gl = jnp.where(lane < N_GROUPS, logits, NEG)
    gmax = jnp.max(gl, axis=1, keepdims=True)
    grp = jnp.min(jnp.where(gl == gmax, lane, float(LANE)), axis=1, keepdims=True)
    p_grp = 1.0 / jnp.sum(jnp.exp(gl - gmax), axis=1, keepdims=True)
    lo = N_GROUPS + N_EXP_PER_GROUP * grp
    el = jnp.where((lane >= lo) & (lane < lo + N_EXP_PER_GROUP), logits, NEG)
    v1 = jnp.max(el, axis=1, keepdims=True)
    i1 = jnp.min(jnp.where(el == v1, lane, float(LANE)), axis=1, keepdims=True)
    el2 = jnp.where(lane == i1, NEG, el)
    v2 = jnp.max(el2, axis=1, keepdims=True)
    i2 = jnp.min(jnp.where(el2 == v2, lane, float(LANE)), axis=1, keepdims=True)
    t = jnp.exp(v2 - v1)
    w1 = p_grp * (1.0 / (1.0 + t))
    w2 = p_grp * (t / (1.0 + t))
    return i1 - N_GROUPS, i2 - N_GROUPS, w1, w2


def _mod_kernel(c_ref, w_ref, b_ref, o_ref):
    o_ref[0] = _dot3(_silu(c_ref[...]), w_ref[0]) + b_ref[0]


def _mod_call(c_all, w_ada, b_ada):
    depth, _, n6 = w_ada.shape
    rows = c_all.shape[0]
    tn = 1024
    return pl.pallas_call(
        _mod_kernel,
        grid=(depth, n6 // tn),
        in_specs=[pl.BlockSpec((rows, D), lambda l, j: (0, 0)),
                  pl.BlockSpec((1, D, tn), lambda l, j: (l, 0, j)),
                  pl.BlockSpec((1, 1, tn), lambda l, j: (l, 0, j))],
        out_specs=pl.BlockSpec((1, rows, tn), lambda l, j: (l, 0, j)),
        out_shape=jax.ShapeDtypeStruct((depth, rows, n6), F32),
        compiler_params=_cparams("parallel", "parallel"),
        name="adaln_mod",
    )(c_all, w_ada, b_ada.reshape(depth, 1, n6))


def _ip1_kernel(x_ref, g_ref, sc_ref, sh_ref, w_ref, cos_ref, sa_ref, sb_ref,
                qa_ref, kaf_ref, kab_ref, vaf_ref, vat_ref,
                qc_ref, kcf_ref, kcb_ref, vcf_ref, vct_ref, h_scr):
    j = pl.program_id(1)
    q_scale = HD ** -0.5 * LOG2E

    @pl.when(j == 0)
    def _():
        h_scr[...] = _prenorm(x_ref[...], g_ref[...], sc_ref[0], sh_ref[0]).astype(BF16)

    z = _dot(h_scr[...], w_ref[...])

    @pl.when(j == 0)
    def _():
        qa_ref[...] = (_rope(z, cos_ref[...], sa_ref[...], sb_ref[...]) * q_scale).astype(BF16)

    @pl.when(j == 1)
    def _():
        k = _rope(z, cos_ref[...], sa_ref[...], sb_ref[...])
        for h in range(H_A):
            kaf_ref[0, h] = k[:, h * 2 * HD:(h + 1) * 2 * HD]
        kab_ref[...] = k.astype(BF16)

    @pl.when(j == 2)
    def _():
        for h in range(H_A):
            vaf_ref[0, h] = z[:, h * 2 * HD:(h + 1) * 2 * HD]
        vat_ref[0] = z.T.astype(BF16)

    @pl.when(j == 3)
    def _():
        qc_ref[...] = (z * q_scale).astype(BF16)

    @pl.when(j == 4)
    def _():
        zt = z.T
        for h in range(H_C):
            kcf_ref[0, h] = zt[h * HD:(h + 1) * HD, :]
        kcb_ref[...] = z.astype(BF16)

    @pl.when(j == 5)
    def _():
        zt = z.T
        for h in range(H_C):
            vcf_ref[0, h] = zt[h * HD:(h + 1) * HD, :]
        vct_ref[0] = zt.astype(BF16)


def _ip1_call(x, g0, sc, sh, w768, rope_tabs, batch, seq):
    n = x.shape[0]
    tm = min(TM_IP, seq)
    tpb = seq // tm
    bmap = lambda i, j: (i // tpb, 0, 0)
    row = pl.BlockSpec((tm, SEG), lambda i, j: (i, 0))
    tab = pl.BlockSpec((tm, LANE), lambda i, j: (i % tpb, 0))
    a_nat = pl.BlockSpec((1, H_A, tm, 2 * HD), lambda i, j: (i // tpb, 0, i % tpb, 0))
    c_nat = pl.BlockSpec((1, H_C, HD, tm), lambda i, j: (i // tpb, 0, 0, i % tpb))
    v_t = pl.BlockSpec((1, SEG, tm), lambda i, j: (i, 0, 0))
    b = jax.ShapeDtypeStruct((n, SEG), BF16)
    fa = jax.ShapeDtypeStruct((batch, H_A, seq, 2 * HD), F32)
    fc = jax.ShapeDtypeStruct((batch, H_C, HD, seq), F32)
    bt = jax.ShapeDtypeStruct((n // tm, SEG, tm), BF16)
    return pl.pallas_call(
        _ip1_kernel,
        grid=(n // tm, N_SEG),
        in_specs=[pl.BlockSpec((tm, D), lambda i, j: (i, 0)),
                  pl.BlockSpec((1, D), lambda i, j: (0, 0)),
                  pl.BlockSpec((1, 1, D), bmap), pl.BlockSpec((1, 1, D), bmap),
                  pl.BlockSpec((D, SEG), lambda i, j: (0, j)),
                  tab, tab, tab],
        out_specs=[row, a_nat, row, a_nat, v_t, row, c_nat, row, c_nat, v_t],
        out_shape=[b, fa, b, fa, bt, b, fc, b, fc, bt],
        scratch_shapes=[pltpu.VMEM((tm, D), BF16)],
        compiler_params=_cparams("parallel", "arbitrary"),
        name="prompt_in_proj",
    )(x, g0, sc, sh, w768, *rope_tabs)


def _ip2_kernel(x_ref, g_ref, sc_ref, sh_ref, w_ref, bf_ref, gs_ref,
                ub_ref, vb_ref, lf_ref, d_ref, carry, *, tpb):
    i = pl.program_id(0)
    tm = x_ref.shape[0]
    h = _prenorm(x_ref[...], g_ref[...], sc_ref[0], sh_ref[0]).astype(BF16)
    z = _dot(h, w_ref[...])
    ub_ref[...] = _gelu(z[:, :DB_W])
    vb_ref[...] = _rms(_gelu(z[:, DB_W:2 * DB_W]), gs_ref[...]).astype(BF16)
    lf = _log_sigmoid(z[:, 2 * DB_W:] + bf_ref[...])
    lf_ref[...] = lf

    @pl.when(i % tpb == 0)
    def _():
        carry[...] = jnp.zeros_like(carry)

    r = lax.broadcasted_iota(I32, (tm, tm), 0)
    c = lax.broadcasted_iota(I32, (tm, tm), 1)
    d = _dot_exact_lhs((r >= c).astype(BF16), lf) + carry[...]
    d_ref[...] = d
    carry[...] = d[tm - 1:tm, :]


def _ip2_call(x, g0, sc, sh, wb, bf, gs, seq):
    n = x.shape[0]
    tm = min(TM_IP, seq)
    tpb = seq // tm
    bmap = lambda i: (i // tpb, 0, 0)
    full = lambda shp: pl.BlockSpec(shp, lambda i: (0,) * len(shp))
    return pl.pallas_call(
        functools.partial(_ip2_kernel, tpb=tpb),
        grid=(n // tm,),
        in_specs=[pl.BlockSpec((tm, D), lambda i: (i, 0)), full((1, D)),
                  pl.BlockSpec((1, 1, D), bmap), pl.BlockSpec((1, 1, D), bmap),
                  full((D, 2 * DB_W + LANE)), full((1, LANE)), full((1, DB_W))],
        out_specs=[pl.BlockSpec((tm, DB_W), lambda i: (i, 0)),
                   pl.BlockSpec((tm, DB_W), lambda i: (i, 0)),
                   pl.BlockSpec((tm, LANE), lambda i: (i, 0)),
                   pl.BlockSpec((tm, LANE), lambda i: (i, 0))],
        out_shape=[jax.ShapeDtypeStruct((n, DB_W), F32), jax.ShapeDtypeStruct((n, DB_W), BF16),
                   jax.ShapeDtypeStruct((n, LANE), F32), jax.ShapeDtypeStruct((n, LANE), F32)],
        scratch_shapes=[pltpu.VMEM((1, LANE), F32)],
        compiler_params=_cparams("arbitrary"),
        name="prompt_sgu_gate_proj",
    )(x, g0, sc, sh, wb, bf, gs)


def _flash_kernel(*refs, mode, tq, lam_init):
    if mode == "a":
        q_ref, k_ref, vt_ref, dl_ref, gs_ref, o_ref, qq, m_s, l_s, acc = refs
    else:
        q_ref, k_ref, vt_ref, dq_ref, dk_ref, o_ref, qq, m_s, l_s, acc = refs
    qi = pl.program_id(2)
    q = q_ref[...]
    lane = lax.broadcasted_iota(I32, q.shape, 1)
    qq[0:tq, :] = jnp.where(lane < HD, q, jnp.zeros_like(q))
    qq[tq:2 * tq, :] = jnp.where(lane >= HD, q, jnp.zeros_like(q))
    m_s[...] = jnp.full_like(m_s, NEG)
    l_s[...] = jnp.zeros_like(l_s)
    acc[...] = jnp.zeros_like(acc)
    if mode == "c":
        dq = dq_ref[0, 0, 0]
        dq_row = jnp.concatenate([dq[0:1, :], dq[1:2, :]], axis=1)

    def step(kb, diagonal):
        k = k_ref[pl.ds(pl.multiple_of(kb * tq, tq), tq), :]
        s = _dot_nt(k, qq[...])
        if mode == "c":
            dk = dk_ref[0, 0, pl.ds(pl.multiple_of(kb * tq, tq), tq), :]
            dk_cols = jnp.concatenate([jnp.broadcast_to(dk[:, 0:1], (tq, tq)),
                                       jnp.broadcast_to(dk[:, 1:2], (tq, tq))], axis=1)
            s = s + (dq_row - dk_cols)
        if diagonal:
            r = lax.broadcasted_iota(I32, (tq, tq), 0)
            c = lax.broadcasted_iota(I32, (tq, tq), 1)
            keep = jnp.concatenate([r <= c, r <= c], axis=1)
            s = jnp.where(keep, s, NEG)
        m_old = m_s[...]
        m_new = jnp.maximum(m_old, jnp.max(s, axis=0, keepdims=True))
        alpha = jnp.exp2(m_old - m_new)
        p = jnp.exp2(s - m_new)
        l_s[...] = alpha * l_s[...] + jnp.sum(p, axis=0, keepdims=True)
        acc[...] = alpha * acc[...] + _dot(vt_ref[kb], p.astype(BF16))
        m_s[...] = m_new

    def body(kb, carry):
        step(kb, False)
        return carry

    lax.fori_loop(0, qi, body, 0)
    step(qi, True)

    o = acc[...] * (1.0 / l_s[...])
    if mode == "a":
        dl = dl_ref[...]
        lam = (jnp.exp(jnp.sum(dl[0:1] * dl[1:2], axis=1, keepdims=True))
               - jnp.exp(jnp.sum(dl[2:3] * dl[3:4], axis=1, keepdims=True)) + lam_init)
        oa = o[:, 0:tq] - lam * o[:, tq:2 * tq]
        oa = oa * lax.rsqrt(jnp.mean(oa * oa, axis=0, keepdims=True) + EPS) * gs_ref[...]
        o_ref[...] = (oa * (1.0 - lam_init)).T.astype(o_ref.dtype)
    else:
        d_row = lax.broadcasted_iota(I32, (2 * HD, tq), 0)
        o_ref[...] = jnp.where(d_row < HD, o[:, 0:tq], o[:, tq:2 * tq]).T


def _flash_call(mode, q, k, v, extra, batch, seq, lam_init=0.0):
    n = q.shape[0]
    tq = min(TQ, seq)
    nq = seq // tq
    qspec = pl.BlockSpec((tq, LANE), lambda b, g, i: (b * nq + i, g))
    kspec = pl.BlockSpec((seq, LANE), lambda b, g, i: (b, g))
    vtspec = pl.BlockSpec((nq, LANE, tq), lambda b, g, i: (b, g, 0))
    if mode == "a":
        especs = [pl.BlockSpec((4, HD), lambda b, g, i: (0, 0)),
                  pl.BlockSpec((LANE, 1), lambda b, g, i: (0, 0))]
        out_dtype = BF16
    else:
        especs = [pl.BlockSpec((1, 1, 1, 2, tq), lambda b, g, i: (b, g, i, 0, 0)),
                  pl.BlockSpec((1, 1, seq, 2), lambda b, g, i: (b, g, 0, 0))]
        out_dtype = F32
    return pl.pallas_call(
        functools.partial(_flash_kernel, mode=mode, tq=tq, lam_init=lam_init),
        grid=(batch, SEG // LANE, nq),
        in_specs=[qspec, kspec, vtspec] + especs,
        out_specs=qspec,
        out_shape=jax.ShapeDtypeStruct((n, SEG), out_dtype),
        scratch_shapes=[pltpu.VMEM((2 * tq, LANE), BF16), pltpu.VMEM((1, 2 * tq), F32),
                        pltpu.VMEM((1, 2 * tq), F32), pltpu.VMEM((LANE, 2 * tq), F32)],
        compiler_params=_cparams("parallel", "parallel", "parallel"),
        name="prompt_flash_" + mode,
    )(q, k, v, *extra)


def _op_kernel(oa_ref, oc_ref, ub_ref, vb_ref, ws_ref, bst_ref, gb_ref, gc_ref, wout_ref,
               x_ref, g1_ref, ga1_ref, g2_ref, sc2_ref, sh2_ref, wr_ref, br_ref,
               x1_ref, h2_ref, eid_ref, ew_ref, cat):
    tm = x_ref.shape[0]
    r = lax.broadcasted_iota(I32, (CHUNK, CHUNK), 0)
    c = lax.broadcasted_iota(I32, (CHUNK, CHUNK), 1)
    lane_b = lax.broadcasted_iota(I32, (CHUNK, DB_W), 1)
    w_tril = [jnp.where(r >= c, ws_ref[g], 0.0).astype(BF16) for g in range(G_B)]
    for ch in range(tm // CHUNK):
        rows = slice(ch * CHUNK, (ch + 1) * CHUNK)
        vb = vb_ref[rows, :]
        mix = bst_ref[...]
        for g in range(G_B):
            in_group = (lane_b >= g * HD) & (lane_b < (g + 1) * HD)
            mix = mix + jnp.where(in_group, _dot(w_tril[g], vb), 0.0)
        cat[rows, SEG:SEG + DB_W] = _rms(ub_ref[rows, :] * mix, gb_ref[...]).astype(BF16)
    cat[:, 0:SEG] = oa_ref[...]
    cat[:, SEG + DB_W:] = _rms(oc_ref[...], gc_ref[...]).astype(BF16)
    m = _dot(cat[...], wout_ref[...])
    x1 = x_ref[...] + ga1_ref[0] * _rms(m, g1_ref[...])
    x1_ref[...] = x1
    h2 = _prenorm(x1, g2_ref[...], sc2_ref[0], sh2_ref[0])
    h2_ref[...] = h2
    e1, e2, w1, w2 = _route(_dot(h2.astype(BF16), wr_ref[...]) + br_ref[...])
    lane = lax.broadcasted_iota(I32, (tm, LANE), 1)
    eid_ref[...] = jnp.where(lane == 0, e1, jnp.where(lane == 1, e2, 0.0)).astype(I32)
    ew_ref[...] = jnp.where(lane == 0, w1, jnp.where(lane == 1, w2, 0.0))


def _op_call(oa, oc, ub, vb, ws, bst, gb, gc, wout, x, g1, ga1, g2, sc2, sh2, wr, br, seq):
    n = x.shape[0]
    tm = min(TM_OP, seq)
    tpb = seq // tm
    bmap = lambda i: (i // tpb, 0, 0)
    full = lambda shp: pl.BlockSpec(shp, lambda i: (0,) * len(shp))
    rows = lambda w: pl.BlockSpec((tm, w), lambda i: (i, 0))
    return pl.pallas_call(
        _op_kernel,
        grid=(n // tm,),
        in_specs=[rows(SEG), rows(SEG), rows(DB_W), rows(DB_W),
                  full((G_B, CHUNK, CHUNK)), full((CHUNK, DB_W)), full((1, DB_W)), full((1, SEG)),
                  full((D, D)), rows(D), full((1, D)), pl.BlockSpec((1, 1, D), bmap),
                  full((1, D)), pl.BlockSpec((1, 1, D), bmap), pl.BlockSpec((1, 1, D), bmap),
                  full((D, LANE)), full((1, LANE))],
        out_specs=[rows(D), rows(D), rows(LANE), rows(LANE)],
        out_shape=[jax.ShapeDtypeStruct((n, D), F32), jax.ShapeDtypeStruct((n, D), F32),
                   jax.ShapeDtypeStruct((n, LANE), I32), jax.ShapeDtypeStruct((n, LANE), F32)],
        scratch_shapes=[pltpu.VMEM((tm, D), BF16)],
        compiler_params=_cparams("parallel"),
        name="prompt_mix_out_router",
    )(oa, oc, ub, vb, ws, bst, gb, gc, wout, x, g1, ga1, g2, sc2, sh2, wr, br)


def _moe_kernel(te_ref, nv_ref, tok_ref, h2_ref, gw_ref, wgu_ref, wd_ref, y_ref,
                xbuf, wgu_b, wd_b, sem):
    t = pl.program_id(0)
    n_valid = nv_ref[0]
    valid = t < n_valid
    slot = lax.rem(t, 2)

    def for_rows(tile, buf_slot, act):
        def body(r, carry):
            act(pltpu.make_async_copy(h2_ref.at[pl.ds(tok_ref[tile * TM_E + r], 1)],
                                      xbuf.at[buf_slot, pl.ds(r, 1)], sem.at[buf_slot]))
            return carry
        lax.fori_loop(0, TM_E, body, 0, unroll=8)

    @pl.when(t == 0)
    def _():
        for_rows(0, 0, lambda cp: cp.start())

    @pl.when(t + 1 < n_valid)
    def _():
        for_rows(t + 1, 1 - slot, lambda cp: cp.start())

    @pl.when(valid & ((t == 0) | (te_ref[t] != te_ref[jnp.maximum(t - 1, 0)])))
    def _():
        wgu_b[...] = wgu_ref[0, 0].astype(BF16)
        wd_b[...] = wd_ref[0, 0].astype(BF16)

    @pl.when(valid)
    def _():
        for_rows(t, slot, lambda cp: cp.wait())
        hu = _dot(xbuf[slot].astype(BF16), wgu_b[...])
        hid = _silu(hu[:, :D_EXP]) * hu[:, D_EXP:] * gw_ref[...]
        y_ref[...] = _dot(hid.astype(BF16), wd_b[...])

    @pl.when(jnp.logical_not(valid))
    def _():
        y_ref[...] = jnp.zeros_like(y_ref)


def _moe_call(layer, tile_e, n_valid, tok_of_slot, h2, gw, w_gu, w_down):
    total = tok_of_slot.shape[0]
    n_tiles = total // TM_E
    clamp = lambda t, te, nv, tok: (jnp.minimum(t, nv[0] - 1), 0)
    expert = lambda t, te, nv, tok: (layer, te[t], 0, 0)
    return pl.pallas_call(
        _moe_kernel,
        grid_spec=pltpu.PrefetchScalarGridSpec(
            num_scalar_prefetch=3, grid=(n_tiles,),
            in_specs=[pl.BlockSpec(memory_space=pl.ANY),
                      pl.BlockSpec((TM_E, 1), clamp),
                      pl.BlockSpec((1, 1, D, 2 * D_EXP), expert),
                      pl.BlockSpec((1, 1, D_EXP, D), expert)],
            out_specs=pl.BlockSpec((TM_E, D), lambda t, te, nv, tok: (t, 0)),
            scratch_shapes=[pltpu.VMEM((2, TM_E, D), F32),
                            pltpu.VMEM((D, 2 * D_EXP), BF16), pltpu.VMEM((D_EXP, D), BF16),
                            pltpu.SemaphoreType.DMA((2,))]),
        out_shape=jax.ShapeDtypeStruct((total, D), F32),
        compiler_params=_cparams("arbitrary"),
        name="moe_experts",
    )(tile_e, n_valid, tok_of_slot, h2, gw, w_gu, w_down)


def _combine_kernel(pos_ref, y_ref, x1_ref, ga_ref, g3_ref, x2_ref, buf, sem, *, n_tok):
    i = pl.program_id(0)
    tm = x1_ref.shape[0]

    def copy(k, row):
        slot = pos_ref[k * n_tok + i * tm + row]
        return pltpu.make_async_copy(y_ref.at[pl.ds(slot, 1)], buf.at[k, pl.ds(row, 1)], sem)

    def issue(row, carry):
        copy(0, row).start()
        copy(1, row).start()
        return carry

    def drain(row, carry):
        copy(0, row).wait()
        copy(1, row).wait()
        return carry

    lax.fori_loop(0, tm, issue, 0)
    lax.fori_loop(0, tm, drain, 0)
    x2_ref[...] = x1_ref[...] + ga_ref[0] * _rms(buf[0] + buf[1], g3_ref[...])


def _combine_call(pos, y, x1, ga2, g3, seq):
    n = x1.shape[0]
    tm = min(TM_CB, seq)
    tpb = seq // tm
    return pl.pallas_call(
        functools.partial(_combine_kernel, n_tok=n),
        grid_spec=pltpu.PrefetchScalarGridSpec(
            num_scalar_prefetch=1, grid=(n // tm,),
            in_specs=[pl.BlockSpec(memory_space=pl.ANY),
                      pl.BlockSpec((tm, D), lambda i, pos: (i, 0)),
                      pl.BlockSpec((1, 1, D), lambda i, pos: (i // tpb, 0, 0)),
                      pl.BlockSpec((1, D), lambda i, pos: (0, 0))],
            out_specs=pl.BlockSpec((tm, D), lambda i, pos: (i, 0)),
            scratch_shapes=[pltpu.VMEM((2, tm, D), F32), pltpu.SemaphoreType.DMA(())]),
        out_shape=jax.ShapeDtypeStruct((n, D), F32),
        compiler_params=_cparams("arbitrary"),
        name="moe_combine",
    )(pos, y, x1, ga2, g3)


def _moe_plan(eid, ew):
    n = eid.shape[0]
    pe = jnp.concatenate([eid[:, 0], eid[:, 1]])
    pw = jnp.concatenate([ew[:, 0], ew[:, 1]])
    tok = jnp.concatenate([jnp.arange(n, dtype=I32)] * 2)
    onehot = (pe[:, None] == jnp.arange(N_EXP, dtype=I32)[None, :]).astype(I32)
    counts = jnp.sum(onehot, axis=0)
    rank = jnp.sum((jnp.cumsum(onehot, axis=0) - onehot) * onehot, axis=1)
    padded = ((counts + TM_E - 1) // TM_E) * TM_E
    ends = jnp.cumsum(padded)
    dest = (ends - padded)[pe] + rank
    total = 2 * n + N_EXP * TM_E
    tok_of_slot = jnp.zeros((total,), I32).at[dest].set(tok)
    gw = jnp.zeros((total,), F32).at[dest].set(pw)
    n_valid = (ends[-1] // TM_E).astype(I32)
    starts = jnp.arange(total // TM_E, dtype=I32) * TM_E
    tile_e = jnp.minimum(jnp.sum((starts[:, None] >= ends[None, :]).astype(I32), axis=1), N_EXP - 1)
    tile_e = jnp.where(starts // TM_E < n_valid, tile_e, tile_e[jnp.maximum(n_valid - 1, 0)])
    return tok_of_slot, gw.reshape(total, 1), tile_e.astype(I32), n_valid.reshape(1), dest.astype(I32)


def _sk1_kernel(x_ref, g_ref, sc_ref, sh_ref, w_ref, wf_ref, z_ref, zf_ref, h_scr):
    j = pl.program_id(0)

    @pl.when(j == 0)
    def _():
        h = _prenorm(x_ref[...], g_ref[...], sc_ref[...], sh_ref[...])
        h_scr[...] = h
        zf_ref[...] = _dot3(h, wf_ref[...])

    z_ref[...] = _dot3(h_scr[...], w_ref[0])


def _sk1_call(layer, x, g0, sc, sh, w_in, wf):
    nb = x.shape[0]
    full = lambda shp: pl.BlockSpec(shp, lambda j: (0,) * len(shp))
    return pl.pallas_call(
        _sk1_kernel,
        grid=(OFF_F // TN_S,),
        in_specs=[full((nb, D)), full((1, D)), full((nb, D)), full((nb, D)),
                  pl.BlockSpec((1, D, TN_S), lambda j: (layer, 0, j)), full((D, LANE))],
        out_specs=[pl.BlockSpec((nb, TN_S), lambda j: (0, j)), full((nb, LANE))],
        out_shape=[jax.ShapeDtypeStruct((nb, OFF_F), F32), jax.ShapeDtypeStruct((nb, LANE), F32)],
        scratch_shapes=[pltpu.VMEM((nb, D), F32)],
        compiler_params=_cparams("arbitrary"),
        name="decode_in_proj",
    )(x, g0, sc, sh, w_in, wf)


def _sk2_kernel(z_ref, zf_ref, cos_ref, sa_ref, sb_ref, bf_ref, gs_ref,
                qa_ref, ka_ref, va_ref, ub_ref, vb_ref, qc_ref, kc_ref, vc_ref, lf_ref):
    c, sa, sb = cos_ref[...], sa_ref[...], sb_ref[...]
    qa_ref[...] = _rope(z_ref[:, 0:SEG], c, sa, sb) * HD ** -0.5
    ka_ref[...] = _rope(z_ref[:, SEG:2 * SEG], c, sa, sb)
    va_ref[...] = z_ref[:, 2 * SEG:OFF_B]
    ub_ref[...] = _gelu(z_ref[:, OFF_B:OFF_B + DB_W])
    vb_ref[...] = _rms(_gelu(z_ref[:, OFF_B + DB_W:OFF_C]), gs_ref[...])
    qc_ref[...] = z_ref[:, OFF_C:OFF_C + SEG] * HD ** -0.5
    kc_ref[...] = z_ref[:, OFF_C + SEG:OFF_C + 2 * SEG]
    vc_ref[...] = z_ref[:, OFF_C + 2 * SEG:OFF_F]
    lf_ref[...] = _log_sigmoid(zf_ref[...] + bf_ref[...])


def _sk2_call(z, zf, rope_tabs, bf, gs):
    nb = z.shape[0]
    s = lambda w: jax.ShapeDtypeStruct((nb, w), F32)
    return pl.pallas_call(
        _sk2_kernel,
        out_shape=[s(SEG), s(SEG), s(SEG), s(DB_W), s(DB_W), s(SEG), s(SEG), s(SEG), s(LANE)],
        name="decode_proj_post",
    )(z, zf, *rope_tabs, bf, gs)


def _lanes_sum_exact(acc):
    ones = jnp.ones((8, acc.shape[1]), BF16)
    h1 = acc.astype(BF16)
    r1 = acc - h1.astype(F32)
    h2 = r1.astype(BF16)
    h3 = (r1 - h2.astype(F32)).astype(BF16)
    y = _dot_nt(ones, h1) + _dot_nt(ones, h2) + _dot_nt(ones, h3)
    return y[0:1]


def _sk3_kernel(pt_ref, qa_ref, kan_ref, van_ref, qc_ref, kcn_ref, qct_ref, vct_ref, lfn_ref, dl_ref,
                *rest, lam_init, n_pg):
    ak, av, ck, cv, clf = (rest[i * n_pg:(i + 1) * n_pg] for i in range(5))
    (oa_ref, oc_ref, qa_s, m_a, l_a, acc_a, qc_s, m_c, l_c, acc_c, carry, s_scr, p_scr) = rest[5 * n_pg:]
    p = pl.program_id(1)
    rows_a = ROWS_A
    wide = H_A * CHUNK
    row = lax.broadcasted_iota(I32, (rows_a, 2 * HD), 0)
    lane = lax.broadcasted_iota(I32, (rows_a, 2 * HD), 1)
    pair = lax.shift_right_logical(row, 1)

    def rows_of(ref):
        out = jnp.zeros((rows_a, 2 * HD), F32)
        for h in range(H_A):
            out = jnp.where(pair == h, jnp.broadcast_to(ref[0, h:h + 1, :], (rows_a, 2 * HD)), out)
        return out

    @pl.when(p == 0)
    def _():
        second_map = lax.bitwise_and(row, 1) == 1
        q16 = jnp.where((lane < HD) != second_map, rows_of(qa_ref), 0.0)
        qa_s[...] = q16
        m_a[...] = jnp.sum(q16 * rows_of(kan_ref), axis=1, keepdims=True)
        l_a[...] = jnp.ones_like(l_a)
        acc_a[...] = rows_of(van_ref)
        m_c[...] = jnp.sum(qc_ref[0] * kcn_ref[0], axis=1, keepdims=True)
        l_c[...] = jnp.ones_like(l_c)
        lane1 = lax.broadcasted_iota(I32, (1, CHUNK), 1)
        for h in range(H_C):
            qc_s[h] = jnp.broadcast_to(qct_ref[0, :, h:h + 1], (HD, CHUNK))
            acc_c[h] = jnp.where(lane1 == 0, jnp.broadcast_to(vct_ref[0, :, h:h + 1], (HD, CHUNK)), 0.0)
        r16 = lax.broadcasted_iota(I32, (16, LANE), 0)
        l16 = lax.broadcasted_iota(I32, (16, LANE), 1)
        carry[...] = jnp.sum(jnp.where(r16 == l16, jnp.broadcast_to(lfn_ref[0], (16, LANE)), 0.0),
                             axis=1, keepdims=True)

    row_w = lax.broadcasted_iota(I32, (rows_a, wide), 0)
    col_w = lax.broadcasted_iota(I32, (rows_a, wide), 1)
    same_head = lax.shift_right_logical(col_w, 7) == lax.shift_right_logical(row_w, 1)
    qb = qa_s[...].astype(BF16)
    s_a = jnp.concatenate(
        [jnp.where(same_head, _dot_nt(qb, ak[k][0, 0].reshape(wide, 2 * HD).astype(BF16)), NEG)
         for k in range(n_pg)], axis=1)
    m_old = m_a[...]
    m_new = jnp.maximum(m_old, jnp.max(s_a, axis=1, keepdims=True))
    alpha = jnp.exp(m_old - m_new)
    pr = jnp.exp(s_a - m_new)
    l_a[...] = alpha * l_a[...] + jnp.sum(pr, axis=1, keepdims=True)
    pv = _dot(pr[:, 0:wide].astype(BF16), av[0][0, 0].reshape(wide, 2 * HD).astype(BF16))
    for k in range(1, n_pg):
        pv = pv + _dot(pr[:, k * wide:(k + 1) * wide].astype(BF16),
                       av[k][0, 0].reshape(wide, 2 * HD).astype(BF16))
    acc_a[...] = alpha * acc_a[...] + pv
    m_a[...] = m_new

    i_ = lax.broadcasted_iota(I32, (CHUNK, CHUNK), 0)
    j_ = lax.broadcasted_iota(I32, (CHUNK, CHUNK), 1)
    later_mask = (i_ > j_).astype(BF16)
    run = carry[...]
    for k in range(n_pg):
        lf = clf[k][0, 0]
        h1 = lf.astype(BF16)
        r1 = lf - h1.astype(F32)
        h2 = r1.astype(BF16)
        h3 = (r1 - h2.astype(F32)).astype(BF16)
        y = _dot(jnp.concatenate([h1, h2, h3], axis=0), later_mask)
        bias = run + (y[0:16] + y[16:32] + y[32:48])
        for h in range(H_C):
            s_scr[h:h + 1, k * CHUNK:(k + 1) * CHUNK] = (
                jnp.sum(ck[k][0, 0, h] * qc_s[h], axis=0, keepdims=True) + bias[h:h + 1, :])
        run = run + jnp.sum(lf, axis=1, keepdims=True)
    carry[...] = run
    s_c = s_scr[...]
    m_old = m_c[...]
    m_new = jnp.maximum(m_old, jnp.max(s_c, axis=1, keepdims=True))
    alpha = jnp.exp(m_old - m_new)
    pr = jnp.exp(s_c - m_new)
    l_c[...] = alpha * l_c[...] + jnp.sum(pr, axis=1, keepdims=True)
    m_c[...] = m_new
    p_scr[...] = pr
    for h in range(H_C):
        acc = alpha[h:h + 1, :] * acc_c[h]
        for k in range(n_pg):
            acc = acc + cv[k][0, 0, h] * p_scr[h:h + 1, k * CHUNK:(k + 1) * CHUNK]
        acc_c[h] = acc

    @pl.when(p == pl.num_programs(1) - 1)
    def _():
        dl = dl_ref[...]
        lam = (jnp.exp(jnp.sum(dl[0:1] * dl[1:2], axis=1, keepdims=True))
               - jnp.exp(jnp.sum(dl[2:3] * dl[3:4], axis=1, keepdims=True)) + lam_init)
        o = acc_a[...] * (1.0 / l_a[...])
        for h in range(H_A):
            oa_ref[0, h:h + 1, :] = o[2 * h:2 * h + 1] - lam * o[2 * h + 1:2 * h + 2]
        inv_l = 1.0 / l_c[...]
        for h in range(H_C):
            oc_ref[0, h:h + 1, :] = _lanes_sum_exact(acc_c[h]) * inv_l[h:h + 1, :]


def _sk3_call(layer, page_table, qa, ka, va, qc, kc, vc, lf, dl, cak, cav, cck, ccv, clf_t, lam_init):
    nb, n_pages = page_table.shape
    n_pg = math.gcd(n_pages, PAGES_PER_STEP)
    heads = lambda a, h: a.reshape(nb, h, a.shape[1] // h)
    qc3, kc3, vc3 = heads(qc, H_C), heads(kc, H_C), heads(vc, H_C)
    per_b = lambda shp: pl.BlockSpec((1,) + shp, lambda b, p, pt: (b, 0, 0))

    def page_spec(shp, k):
        def index(b, p, pt):
            return (layer, pt[b * n_pages + n_pages - 1 - (p * n_pg + k)]) + (0,) * len(shp)
        return pl.BlockSpec((1, 1) + shp, index)

    pages = ([page_spec((H_A, CHUNK, 2 * HD), k) for k in range(n_pg)] * 2
             + [page_spec((H_C, HD, CHUNK), k) for k in range(n_pg)] * 2
             + [page_spec((16, CHUNK), k) for k in range(n_pg)])
    rows_a = ROWS_A
    return pl.pallas_call(
        functools.partial(_sk3_kernel, lam_init=lam_init, n_pg=n_pg),
        grid_spec=pltpu.PrefetchScalarGridSpec(
            num_scalar_prefetch=1, grid=(nb, n_pages // n_pg),
            in_specs=[per_b((H_A, 2 * HD)), per_b((H_A, 2 * HD)), per_b((H_A, 2 * HD)),
                      per_b((H_C, HD)), per_b((H_C, HD)), per_b((HD, H_C)), per_b((HD, H_C)),
                      per_b((1, LANE)),
                      pl.BlockSpec((4, HD), lambda b, p, pt: (0, 0))] + pages,
            out_specs=[per_b((H_A, 2 * HD)), per_b((H_C, HD))],
            scratch_shapes=[pltpu.VMEM((rows_a, 2 * HD), F32), pltpu.VMEM((rows_a, 1), F32),
                            pltpu.VMEM((rows_a, 1), F32), pltpu.VMEM((rows_a, 2 * HD), F32),
                            pltpu.VMEM((H_C, HD, CHUNK), F32), pltpu.VMEM((H_C, 1), F32),
                            pltpu.VMEM((H_C, 1), F32), pltpu.VMEM((H_C, HD, CHUNK), F32),
                            pltpu.VMEM((16, 1), F32),
                            pltpu.VMEM((H_C, n_pg * CHUNK), F32), pltpu.VMEM((H_C, n_pg * CHUNK), F32)]),
        out_shape=[jax.ShapeDtypeStruct((nb, H_A, 2 * HD), F32), jax.ShapeDtypeStruct((nb, H_C, HD), F32)],
        compiler_params=_cparams("parallel", "arbitrary"),
        name="decode_paged_attention",
    )(page_table.reshape(-1), heads(qa, H_A), heads(ka, H_A), heads(va, H_A),
      qc3, kc3, jnp.swapaxes(qc3, 1, 2), jnp.swapaxes(vc3, 1, 2), lf.reshape(nb, 1, LANE), dl,
      *([cak] * n_pg), *([cav] * n_pg), *([cck] * n_pg), *([ccv] * n_pg), *([clf_t] * n_pg))


def _sk4_kernel(oa_ref, oc_ref, ub_ref, vb_ref, ws0_ref, bs0_ref, gsub_ref, gb_ref, gc_ref, w_ref,
                m_ref, cat, *, lam_init):
    j = pl.program_id(0)

    @pl.when(j == 0)
    def _():
        oa = oa_ref[...]
        parts = []
        for h in range(H_A):
            parts.append(_rms(oa[:, h * 2 * HD:(h + 1) * 2 * HD], gsub_ref[...]) * (1.0 - lam_init))
        parts.append(_rms(ub_ref[...] * (ws0_ref[...] * vb_ref[...] + bs0_ref[...]), gb_ref[...]))
        parts.append(_rms(oc_ref[...], gc_ref[...]))
        cat[...] = jnp.concatenate(parts, axis=1)

    m_ref[...] = _dot3(cat[...], w_ref[0])


def _sk4_call(layer, oa, oc, ub, vb, ws0, bs0, gsub, gb, gc, w_out, lam_init):
    nb = oa.shape[0]
    full = lambda shp: pl.BlockSpec(shp, lambda j: (0,) * len(shp))
    return pl.pallas_call(
        functools.partial(_sk4_kernel, lam_init=lam_init),
        grid=(D // TN_S,),
        in_specs=[full((nb, SEG)), full((nb, SEG)), full((nb, DB_W)), full((nb, DB_W)),
                  full((1, DB_W)), full((1, DB_W)), full((1, 2 * HD)), full((1, DB_W)), full((1, SEG)),
                  pl.BlockSpec((1, D, TN_S), lambda j: (layer, 0, j))],
        out_specs=pl.BlockSpec((nb, TN_S), lambda j: (0, j)),
        out_shape=jax.ShapeDtypeStruct((nb, D), F32),
        scratch_shapes=[pltpu.VMEM((nb, D), F32)],
        compiler_params=_cparams("arbitrary"),
        name="decode_out_proj",
    )(oa, oc, ub, vb, ws0, bs0, gsub, gb, gc, w_out)


def _sk4b_kernel(m_ref, x_ref, g1_ref, ga1_ref, g2_ref, sc2_ref, sh2_ref, wr_ref, br_ref,
                 x1_ref, h2_ref, gate_ref):
    x1 = x_ref[...] + ga1_ref[...] * _rms(m_ref[...], g1_ref[...])
    x1_ref[...] = x1
    h2 = _prenorm(x1, g2_ref[...], sc2_ref[...], sh2_ref[...])
    h2_ref[...] = h2
    e1, e2, w1, w2 = _route(_dot3(h2, wr_ref[...]) + br_ref[...])
    lane = lax.broadcasted_iota(I32, gate_ref.shape, 1).astype(F32)
    gate_ref[...] = jnp.where(lane == e1, w1, 0.0) + jnp.where(lane == e2, w2, 0.0)


def _sk4b_call(m, x, g1, ga1, g2, sc2, sh2, wr, br):
    nb = x.shape[0]
    return pl.pallas_call(
        _sk4b_kernel,
        out_shape=[jax.ShapeDtypeStruct((nb, D), F32), jax.ShapeDtypeStruct((nb, D), F32),
                   jax.ShapeDtypeStruct((nb, LANE), F32)],
        name="decode_post_router",
    )(m, x, g1, ga1, g2, sc2, sh2, wr, br)


def _sk5_kernel(el_ref, cnt_ref, h2_ref, gate_ref, x1_ref, ga_ref, g3_ref, wgu_ref, wd_ref,
                x2_ref, acc):
    s = pl.program_id(0)

    @pl.when(s == 0)
    def _():
        acc[...] = jnp.zeros_like(acc)

    @pl.when(s < cnt_ref[0])
    def _():
        hu = _dot3(h2_ref[...], wgu_ref[0, 0])
        lane = lax.broadcasted_iota(I32, gate_ref.shape, 1)
        g = jnp.sum(jnp.where(lane == el_ref[s], gate_ref[...], 0.0), axis=1, keepdims=True)
        hid = _silu(hu[:, :D_EXP]) * hu[:, D_EXP:] * g
        acc[...] += _dot3(hid, wd_ref[0, 0])

    @pl.when(s == pl.num_programs(0) - 1)
    def _():
        x2_ref[...] = x1_ref[...] + ga_ref[...] * _rms(acc[...], g3_ref[...])


def _sk5_call(layer, elist, cnt, h2, gate, x1, ga2, g3, w_gu, w_down):
    nb = h2.shape[0]
    full = lambda shp: pl.BlockSpec(shp, lambda s, el, cnt: (0,) * len(shp))
    return pl.pallas_call(
        _sk5_kernel,
        grid_spec=pltpu.PrefetchScalarGridSpec(
            num_scalar_prefetch=2, grid=(N_EXP,),
            in_specs=[full((nb, D)), full((nb, LANE)), full((nb, D)), full((nb, D)), full((1, D)),
                      pl.BlockSpec((1, 1, D, 2 * D_EXP), lambda s, el, cnt: (layer, el[s], 0, 0)),
                      pl.BlockSpec((1, 1, D_EXP, D), lambda s, el, cnt: (layer, el[s], 0, 0))],
            out_specs=full((nb, D)),
            scratch_shapes=[pltpu.VMEM((nb, D), F32)]),
        out_shape=jax.ShapeDtypeStruct((nb, D), F32),
        compiler_params=_cparams("arbitrary"),
        name="decode_moe",
    )(elist, cnt, h2, gate, x1, ga2, g3, w_gu, w_down)


def _rope_tables(pos):
    half = ROT_DIM // 2
    n = pos.shape[0]
    inv = ROPE_THETA ** (-jnp.arange(0, ROT_DIM, 2, dtype=F32) / ROT_DIM)
    ang = pos.astype(F32)[:, None] * inv[None, :]
    cos, sin = jnp.cos(ang), jnp.sin(ang)
    one = jnp.ones((n, HD - ROT_DIM), F32)
    zero = jnp.zeros((n, HD - ROT_DIM), F32)
    z8 = jnp.zeros((n, half), F32)
    c64 = jnp.concatenate([cos, cos, one], axis=1)
    sa64 = jnp.concatenate([-sin, z8, zero], axis=1)
    sb64 = jnp.concatenate([z8, sin, zero], axis=1)
    rep = lambda t: jnp.concatenate([t, t], axis=1)
    return rep(c64), rep(sa64), rep(sb64)


def _pad_lanes(a):
    return jnp.pad(a, ((0, 0), (0, LANE - a.shape[1])))


def kernel(x_prompt, x_sample, c_prompt, c_sample, cache_a_k, cache_a_v, cache_c_k, cache_c_v,
           cache_c_logf, page_table, w_ada, b_ada, g_norm, w_in, b_f, g_sgu, w_s, b_s,
           diff_lambda, g_subln, g_out_b, g_out_c, w_out, w_rg, b_rg, w_re, b_re, w_gu, w_down):
    depth = w_ada.shape[0]
    batch, seq, _ = x_prompt.shape
    nb, dec_len, _ = x_sample.shape
    assert dec_len == 1 and seq % CHUNK == 0
    n_pool, page = cache_a_k.shape[1], cache_a_k.shape[2]
    assert page == CHUNK
    past = page_table.shape[1] * page
    n = batch * seq

    rows_c = 16
    c_all = jnp.zeros((rows_c, D), F32).at[:batch].set(c_prompt).at[batch:batch + nb].set(c_sample)
    mod = _mod_call(c_all, w_ada, b_ada)

    tabs_p = _rope_tables(jnp.arange(seq))
    tabs_s = _rope_tables(jnp.full((1,), past))
    cak = jnp.transpose(cache_a_k, (0, 1, 3, 2, 4))
    cav = jnp.transpose(cache_a_v, (0, 1, 3, 2, 4))
    cck = jnp.transpose(cache_c_k, (0, 1, 3, 4, 2))
    ccv = jnp.transpose(cache_c_v, (0, 1, 3, 4, 2))
    clf_t = jnp.pad(jnp.swapaxes(cache_c_logf, 2, 3), ((0, 0), (0, 0), (0, 16 - H_C), (0, 0)))

    tq = min(TQ, seq)
    nq = seq // tq
    xp = x_prompt.reshape(n, D)
    xs = x_sample.reshape(nb, D)
    outs = [[] for _ in range(11)]
    for l in range(depth):
        lam_init = 0.8 - 0.6 * math.exp(-0.3 * l)
        row = lambda v: v.reshape(1, -1)
        gn = [row(g_norm[l, k]) for k in range(4)]
        mp = [mod[l, :batch, k * D:(k + 1) * D].reshape(batch, 1, D) for k in range(6)]
        ms = [mod[l, batch:batch + nb, k * D:(k + 1) * D] for k in range(6)]
        wl = w_in[l]
        w768 = jnp.concatenate([wl[:, :OFF_B], wl[:, OFF_C:OFF_F]], axis=1).astype(BF16)
        wf = _pad_lanes(wl[:, OFF_F:])
        wb = jnp.concatenate([wl[:, OFF_B:OFF_C], wf], axis=1).astype(BF16)
        bf = _pad_lanes(row(b_f[l]))
        gs = row(g_sgu[l])
        wr = _pad_lanes(jnp.concatenate([w_rg[l], w_re[l]], axis=1))
        br = _pad_lanes(row(jnp.concatenate([b_rg[l], b_re[l]])))
        gsub, gb, gc = row(g_subln[l]), row(g_out_b[l]), row(g_out_c[l])
        dl = diff_lambda[l]

        qa, kaf, kab, vaf, vat, qc, kcf, kcb, vcf, vct = _ip1_call(xp, gn[0], mp[1], mp[0], w768, tabs_p,
                                                                   batch, seq)
        ub, vb, lfp, dcum = _ip2_call(xp, gn[0], mp[1], mp[0], wb, bf, gs, seq)
        oa = _flash_call("a", qa, kab, vat, (dl, gsub.reshape(2 * HD, 1)), batch, seq, lam_init)
        dh = (dcum[:, :H_C] * LOG2E).reshape(batch, nq, tq, H_C // 2, 2)
        dq = jnp.transpose(dh, (0, 3, 1, 4, 2))
        dk = jnp.transpose(dh, (0, 3, 1, 2, 4)).reshape(batch, H_C // 2, seq, 2)
        oc = _flash_call("c", qc, kcb, vct, (dq, dk), batch, seq)
        bst = jnp.repeat(jnp.swapaxes(b_s[l], 0, 1), HD, axis=1)
        x1, h2, eid, ew = _op_call(oa, oc, ub, vb, w_s[l], bst, gb, gc, w_out[l].astype(BF16),
                                   xp, gn[1], mp[2], gn[2], mp[4], mp[3], wr.astype(BF16), br, seq)
        tok_of_slot, gw, tile_e, n_valid, dest = _moe_plan(eid[:, :2], ew[:, :2])
        y = _moe_call(l, tile_e, n_valid, tok_of_slot, h2, gw, w_gu, w_down)
        xp = _combine_call(dest, y, x1, mp[5], gn[3], seq)
        for k, a in enumerate((kaf, vaf, kcf, vcf, lfp[:, :H_C])):
            outs[k].append(a)

        z, zf = _sk1_call(l, xs, gn[0], ms[1], ms[0], w_in, wf)
        qa_s, ka_s, va_s, ub_s, vb_s, qc_s, kc_s, vc_s, lf_s = _sk2_call(z, zf, tabs_s, bf, gs)
        oa_s, oc_s = _sk3_call(l, page_table, qa_s, ka_s, va_s, qc_s, kc_s, vc_s, lf_s, dl,
                               cak, cav, cck, ccv, clf_t, lam_init)
        ws0 = row(jnp.repeat(w_s[l, :, 0, 0], HD))
        bs0 = row(jnp.repeat(b_s[l, :, 0], HD))
        m_s = _sk4_call(l, oa_s.reshape(nb, SEG), oc_s.reshape(nb, SEG), ub_s, vb_s, ws0, bs0,
                        gsub, gb, gc, w_out, lam_init)
        x1_s, h2_s, gate = _sk4b_call(m_s, xs, gn[1], ms[2], gn[2], ms[4], ms[3], wr, br)
        used = jnp.any(gate[:, :N_EXP] > 0.0, axis=0)
        order = jnp.argsort(jnp.logical_not(used), stable=True).astype(I32)
        cnt = jnp.sum(used).astype(I32)
        elist = jnp.where(jnp.arange(N_EXP) < cnt, order, order[jnp.maximum(cnt - 1, 0)])
        xs = _sk5_call(l, elist, cnt.reshape(1), h2_s, gate, x1_s, ms[5], gn[3], w_gu, w_down)
        for k, a in enumerate((ka_s, va_s, kc_s, vc_s, lf_s[:, :H_C], vb_s)):
            outs[5 + k].append(a)

    st = lambda k: jnp.stack(outs[k])
    return (xp.reshape(batch, seq, D), xs.reshape(nb, 1, D),
            jnp.transpose(st(0), (0, 1, 3, 2, 4)), jnp.transpose(st(1), (0, 1, 3, 2, 4)),
            jnp.transpose(st(2), (0, 1, 4, 2, 3)), jnp.transpose(st(3), (0, 1, 4, 2, 3)),
            st(4).reshape(depth, batch, seq, H_C),
            st(5).reshape(depth, nb, 1, H_A, 2 * HD), st(6).reshape(depth, nb, 1, H_A, 2 * HD),
            st(7).reshape(depth, nb, 1, H_C, HD), st(8).reshape(depth, nb, 1, H_C, HD),
            st(9).reshape(depth, nb, 1, H_C), st(10).reshape(depth, nb, 1, G_B, HD))
```

```python
import functools
import math

import jax
import jax.numpy as jnp
from jax import lax
from jax.experimental import pallas as pl
from jax.experimental.pallas import tpu as pltpu

F32 = jnp.float32
BF16 = jnp.bfloat16
I32 = jnp.int32

D = 2048
HD = 64
H_A = 6
G_B = 8
DB_W = 512
CHUNK = 128
H_C = 12
SEG = 768
N_SEG = 6
OFF_B = 3 * SEG
OFF_C = OFF_B + 2 * DB_W
OFF_F = OFF_C + 3 * SEG
ROT_DIM = 16
ROPE_THETA = 500000.0
N_GROUPS = 4
N_EXP_PER_GROUP = 4
N_EXP = 16
D_EXP = 512
EPS = 1e-6
LANE = 128
NEG = -1e30
LOG2E = 1.4426950408889634
VMEM_LIMIT = 56 * 1024 * 1024

TM_IP = 512
TQ = 1024
TM_OP = 256
TM_E = 256
TM_CB = 256
TN_S = 512
PAGES_PER_STEP = 8
ROWS_A = 16


def _cparams(*sem):
    return pltpu.CompilerParams(dimension_semantics=sem, vmem_limit_bytes=VMEM_LIMIT)


def _dot(a, b):
    return jnp.dot(a, b, preferred_element_type=F32)


def _dot_nt(a, b):
    return lax.dot_general(a, b, (((1,), (1,)), ((), ())), preferred_element_type=F32)


def _split2(x):
    hi = x.astype(BF16)
    lo = (x - hi.astype(F32)).astype(BF16)
    return hi, lo


def _dot3(x, w):
    m = x.shape[0]
    xh, xl = _split2(x)
    wh, wl = _split2(w)
    y = _dot(jnp.concatenate([xh, xl], axis=0), wh)
    return y[:m] + y[m:] + _dot(xh, wl)


def _dot_exact_lhs(a_bf16, x):
    n = x.shape[1]
    h1 = x.astype(BF16)
    r1 = x - h1.astype(F32)
    h2 = r1.astype(BF16)
    h3 = (r1 - h2.astype(F32)).astype(BF16)
    y = _dot(a_bf16, jnp.concatenate([h1, h2, h3], axis=1))
    return y[:, :n] + y[:, n:2 * n] + y[:, 2 * n:]


def _rms(x, g):
    return x * lax.rsqrt(jnp.mean(x * x, axis=-1, keepdims=True) + EPS) * g


def _prenorm(x, g, sc, sh):
    return _rms(x, g) * (1.0 + sc) + sh


def _silu(x):
    return x * (1.0 / (1.0 + jnp.exp(-x)))


def _gelu(x):
    return x * (0.5 * (1.0 + jnp.tanh(0.7978845608028654 * (x + 0.044715 * (x * x * x)))))


def _log_sigmoid(x):
    return jnp.minimum(x, 0.0) - jnp.log1p(jnp.exp(-jnp.abs(x)))


def _rope(z, c, sa, sb):
    outs = []
    for k in range(z.shape[1] // LANE):
        zz = z[:, k * LANE:(k + 1) * LANE]
        outs.append(zz * c + pltpu.roll(zz, LANE - ROT_DIM // 2, 1) * sa
                    + pltpu.roll(zz, ROT_DIM // 2, 1) * sb)
    return jnp.concatenate(outs, axis=1)


def _route(logits):
    lane = lax.broadcasted_iota(I32, logits.shape, 1).astype(F32)
    gl = jnp.where(lane < N_GROUPS, logits, NEG)
    gmax = jnp.max(gl, axis=1, keepdims=True)
    grp = jnp.min(jnp.where(gl == gmax, lane, float(LANE)), axis=1, keepdims=True)
    p_grp = 1.0 / jnp.sum(jnp.exp(gl - gmax), axis=1, keepdims=True)
    lo = N_GROUPS + N_EXP_PER_GROUP * grp
    el = jnp.where((lane >= lo) & (lane < lo + N_EXP_PER_GROUP), logits, NEG)
    v1 = jnp.max(el, axis=1, keepdims=True)
    i1 = jnp.min(jnp.where(el == v1, lane, float(LANE)), axis=1, keepdims=True)
    el2 = jnp.where(lane == i1, NEG, el)
    v2 = jnp.max(el2, axis=1, keepdims=True)
    i2 = jnp.min(jnp.where(el2 == v2, lane, float(LANE)), axis=1, keepdims=True)
    t = jnp.exp(v2 - v1)
    w1 = p_grp * (1.0 / (1.0 + t))
    w2 = p_grp * (t / (1.0 + t))
    return i1 - N_GROUPS, i2 - N_GROUPS, w1, w2


def _mod_kernel(c_ref, w_ref, b_ref, o_ref):
    o_ref[0] = _dot3(_silu(c_ref[...]), w_ref[0]) + b_ref[0]


def _mod_call(c_all, w_ada, b_ada):
    depth, _, n6 = w_ada.shape
    rows = c_all.shape[0]
    tn = 1024
    return pl.pallas_call(
        _mod_kernel,
        grid=(depth, n6 // tn),
        in_specs=[pl.BlockSpec((rows, D), lambda l, j: (0, 0)),
                  pl.BlockSpec((1, D, tn), lambda l, j: (l, 0, j)),
                  pl.BlockSpec((1, 1, tn), lambda l, j: (l, 0, j))],
        out_specs=pl.BlockSpec((1, rows, tn), lambda l, j: (l, 0, j)),
        out_shape=jax.ShapeDtypeStruct((depth, rows, n6), F32),
        compiler_params=_cparams("parallel", "parallel"),
        name="adaln_mod",
    )(c_all, w_ada, b_ada.reshape(depth, 1, n6))


def _ip1_kernel(x_ref, g_ref, sc_ref, sh_ref, w_ref, cos_ref, sa_ref, sb_ref, *rest):
    (qa_ref, kaf_ref, kab_ref, vaf_ref, vat_ref,
     qc_ref, kcf_ref, kcb_ref, vcf_ref, vct_ref, h_scr) = rest[-11:]
    j = pl.program_id(1)
    q_scale = HD ** -0.5 * LOG2E

    @pl.when(j == 0)
    def _():
        h_scr[...] = _prenorm(x_ref[...], g_ref[...], sc_ref[0], sh_ref[0]).astype(BF16)

    z = _dot(h_scr[...], w_ref[...])

    @pl.when(j == 0)
    def _():
        qa_ref[...] = (_rope(z, cos_ref[...], sa_ref[...], sb_ref[...]) * q_scale).astype(BF16)

    @pl.when(j == 1)
    def _():
        k = _rope(z, cos_ref[...], sa_ref[...], sb_ref[...])
        for h in range(H_A):
            kaf_ref[0, 0, h] = k[:, h * 2 * HD:(h + 1) * 2 * HD]
        kab_ref[...] = k.astype(BF16)

    @pl.when(j == 2)
    def _():
        for h in range(H_A):
            vaf_ref[0, 0, h] = z[:, h * 2 * HD:(h + 1) * 2 * HD]
        vat_ref[0] = z.T.astype(BF16)

    @pl.when(j == 3)
    def _():
        qc_ref[...] = (z * q_scale).astype(BF16)

    @pl.when(j == 4)
    def _():
        zt = z.T
        for h in range(H_C):
            kcf_ref[0, 0, h] = zt[h * HD:(h + 1) * HD, :]
        kcb_ref[...] = z.astype(BF16)

    @pl.when(j == 5)
    def _():
        zt = z.T
        for h in range(H_C):
            vcf_ref[0, 0, h] = zt[h * HD:(h + 1) * HD, :]
        vct_ref[0] = zt.astype(BF16)


def _ip1_call(x, g0, sc, sh, w768, rope_tabs, batch, seq, layer, depth, stacked):
    n = x.shape[0]
    tm = min(TM_IP, seq)
    tpb = seq // tm
    bmap = lambda i, j: (i // tpb, 0, 0)
    row = pl.BlockSpec((tm, SEG), lambda i, j: (i, 0))
    tab = pl.BlockSpec((tm, LANE), lambda i, j: (i % tpb, 0))
    a_nat = pl.BlockSpec((1, 1, H_A, tm, 2 * HD), lambda i, j: (layer, i // tpb, 0, i % tpb, 0))
    c_nat = pl.BlockSpec((1, 1, H_C, HD, tm), lambda i, j: (layer, i // tpb, 0, 0, i % tpb))
    v_t = pl.BlockSpec((1, SEG, tm), lambda i, j: (i, 0, 0))
    b = jax.ShapeDtypeStruct((n, SEG), BF16)
    fa = jax.ShapeDtypeStruct((depth, batch, H_A, seq, 2 * HD), F32)
    fc = jax.ShapeDtypeStruct((depth, batch, H_C, HD, seq), F32)
    stacked = () if stacked is None else tuple(stacked)
    n_in = 8
    bt = jax.ShapeDtypeStruct((n // tm, SEG, tm), BF16)
    return pl.pallas_call(
        _ip1_kernel,
        grid=(n // tm, N_SEG),
        in_specs=[pl.BlockSpec((tm, D), lambda i, j: (i, 0)),
                  pl.BlockSpec((1, D), lambda i, j: (0, 0)),
                  pl.BlockSpec((1, 1, D), bmap), pl.BlockSpec((1, 1, D), bmap),
                  pl.BlockSpec((D, SEG), lambda i, j: (0, j)),
                  tab, tab, tab] + [pl.BlockSpec(memory_space=pl.ANY)] * len(stacked),
        out_specs=[row, a_nat, row, a_nat, v_t, row, c_nat, row, c_nat, v_t],
        out_shape=[b, fa, b, fa, bt, b, fc, b, fc, bt],
        input_output_aliases={n_in + k: o for k, o in enumerate((1, 3, 6, 8)[:len(stacked)])},
        scratch_shapes=[pltpu.VMEM((tm, D), BF16)],
        compiler_params=_cparams("parallel", "arbitrary"),
        name="prompt_in_proj",
    )(x, g0, sc, sh, w768, *rope_tabs, *stacked)


def _ip2_kernel(x_ref, g_ref, sc_ref, sh_ref, w_ref, bf_ref, gs_ref,
                ub_ref, vb_ref, lf_ref, d_ref, carry, *, tpb):
    i = pl.program_id(0)
    tm = x_ref.shape[0]
    h = _prenorm(x_ref[...], g_ref[...], sc_ref[0], sh_ref[0]).astype(BF16)
    z = _dot(h, w_ref[...])
    ub_ref[...] = _gelu(z[:, :DB_W])
    vb_ref[...] = _rms(_gelu(z[:, DB_W:2 * DB_W]), gs_ref[...]).astype(BF16)
    lf = _log_sigmoid(z[:, 2 * DB_W:] + bf_ref[...])
    lf_ref[...] = lf

    @pl.when(i % tpb == 0)
    def _():
        carry[...] = jnp.zeros_like(carry)

    r = lax.broadcasted_iota(I32, (tm, tm), 0)
    c = lax.broadcasted_iota(I32, (tm, tm), 1)
    d = _dot_exact_lhs((r >= c).astype(BF16), lf) + carry[...]
    d_ref[...] = d
    carry[...] = d[tm - 1:tm, :]


def _ip2_call(x, g0, sc, sh, wb, bf, gs, seq):
    n = x.shape[0]
    tm = min(TM_IP, seq)
    tpb = seq // tm
    bmap = lambda i: (i // tpb, 0, 0)
    full = lambda shp: pl.BlockSpec(shp, lambda i: (0,) * len(shp))
    return pl.pallas_call(
        functools.partial(_ip2_kernel, tpb=tpb),
        grid=(n // tm,),
        in_specs=[pl.BlockSpec((tm, D), lambda i: (i, 0)), full((1, D)),
                  pl.BlockSpec((1, 1, D), bmap), pl.BlockSpec((1, 1, D), bmap),
                  full((D, 2 * DB_W + LANE)), full((1, LANE)), full((1, DB_W))],
        out_specs=[pl.BlockSpec((tm, DB_W), lambda i: (i, 0)),
                   pl.BlockSpec((tm, DB_W), lambda i: (i, 0)),
                   pl.BlockSpec((tm, LANE), lambda i: (i, 0)),
                   pl.BlockSpec((tm, LANE), lambda i: (i, 0))],
        out_shape=[jax.ShapeDtypeStruct((n, DB_W), F32), jax.ShapeDtypeStruct((n, DB_W), BF16),
                   jax.ShapeDtypeStruct((n, LANE), F32), jax.ShapeDtypeStruct((n, LANE), F32)],
        scratch_shapes=[pltpu.VMEM((1, LANE), F32)],
        compiler_params=_cparams("arbitrary"),
        name="prompt_sgu_gate_proj",
    )(x, g0, sc, sh, wb, bf, gs)


def _flash_kernel(*refs, mode, tq, tv, lam_init):
    if mode == "a":
        q_ref, k_ref, vt_ref, dl_ref, gs_ref, o_ref, qq, m_s, acc = refs
    else:
        q_ref, k_ref, vt_ref, dq_ref, dk_ref, o_ref, qq, m_s, acc = refs
    qi = pl.program_id(2)
    q = q_ref[...]
    lane = lax.broadcasted_iota(I32, q.shape, 1)
    qq[0:tq, :] = jnp.where(lane < HD, q, jnp.zeros_like(q))
    qq[tq:2 * tq, :] = jnp.where(lane >= HD, q, jnp.zeros_like(q))
    m_s[...] = jnp.full_like(m_s, NEG)
    acc[...] = jnp.zeros_like(acc)
    ones = jnp.ones((16, tv), BF16)
    if mode == "c":
        dq = dq_ref[0, 0, 0]
        dq_row = jnp.concatenate([dq[0:1, :], dq[1:2, :]], axis=1)

    def step(kb, diagonal):
        k = k_ref[pl.ds(pl.multiple_of(kb * tq, tq), tq), :]
        s = _dot_nt(k, qq[...])
        if mode == "c":
            dk = dk_ref[0, 0, pl.ds(pl.multiple_of(kb * tq, tq), tq), :]
            dk_cols = jnp.concatenate([jnp.broadcast_to(dk[:, 0:1], (tq, tq)),
                                       jnp.broadcast_to(dk[:, 1:2], (tq, tq))], axis=1)
            s = s + (dq_row - dk_cols)
        if diagonal:
            r = lax.broadcasted_iota(I32, (tq, tq), 0)
            c = lax.broadcasted_iota(I32, (tq, tq), 1)
            keep = jnp.concatenate([r <= c, r <= c], axis=1)
            s = jnp.where(keep, s, NEG)
        m_old = m_s[...]
        m_new = jnp.maximum(m_old, jnp.max(s, axis=0, keepdims=True))
        alpha = jnp.exp2(m_old - m_new)
        p = jnp.exp2(s - m_new)
        pb = p.astype(BF16)
        pv = _dot(jnp.concatenate([vt_ref[kb * (tq // tv)], ones], axis=0), pb[0:tv])
        for j in range(1, tq // tv):
            pv = pv + _dot(jnp.concatenate([vt_ref[kb * (tq // tv) + j], ones], axis=0),
                           pb[j * tv:(j + 1) * tv])
        acc[...] = alpha * acc[...] + pv
        m_s[...] = m_new

    def body(kb, carry):
        step(kb, False)
        return carry

    lax.fori_loop(0, qi, body, 0)
    step(qi, True)

    o = acc[0:LANE, :] * (1.0 / acc[LANE:LANE + 1, :])
    if mode == "a":
        dl = dl_ref[...]
        lam = (jnp.exp(jnp.sum(dl[0:1] * dl[1:2], axis=1, keepdims=True))
               - jnp.exp(jnp.sum(dl[2:3] * dl[3:4], axis=1, keepdims=True)) + lam_init)
        oa = o[:, 0:tq] - lam * o[:, tq:2 * tq]
        oa = oa * lax.rsqrt(jnp.mean(oa * oa, axis=0, keepdims=True) + EPS) * gs_ref[...]
        o_ref[...] = (oa * (1.0 - lam_init)).T.astype(o_ref.dtype)
    else:
        d_row = lax.broadcasted_iota(I32, (2 * HD, tq), 0)
        o_ref[...] = jnp.where(d_row < HD, o[:, 0:tq], o[:, tq:2 * tq]).T


def _flash_call(mode, q, k, v, extra, batch, seq, lam_init=0.0):
    n = q.shape[0]
    tq = min(TQ, seq)
    tv = min(TM_IP, seq)
    nq = seq // tq
    qspec = pl.BlockSpec((tq, LANE), lambda b, g, i: (b * nq + i, g))
    kspec = pl.BlockSpec((seq, LANE), lambda b, g, i: (b, g))
    vtspec = pl.BlockSpec((seq // tv, LANE, tv), lambda b, g, i: (b, g, 0))
    if mode == "a":
        especs = [pl.BlockSpec((4, HD), lambda b, g, i: (0, 0)),
                  pl.BlockSpec((LANE, 1), lambda b, g, i: (0, 0))]
        out_dtype = BF16
    else:
        especs = [pl.BlockSpec((1, 1, 1, 2, tq), lambda b, g, i: (b, g, i, 0, 0)),
                  pl.BlockSpec((1, 1, seq, 2), lambda b, g, i: (b, g, 0, 0))]
        out_dtype = F32
    return pl.pallas_call(
        functools.partial(_flash_kernel, mode=mode, tq=tq, tv=tv, lam_init=lam_init),
        grid=(batch, SEG // LANE, nq),
        in_specs=[qspec, kspec, vtspec] + especs,
        out_specs=qspec,
        out_shape=jax.ShapeDtypeStruct((n, SEG), out_dtype),
        scratch_shapes=[pltpu.VMEM((2 * tq, LANE), BF16), pltpu.VMEM((1, 2 * tq), F32),
                        pltpu.VMEM((LANE + 16, 2 * tq), F32)],
        compiler_params=_cparams("parallel", "parallel", "parallel"),
        name="prompt_flash_" + mode,
    )(q, k, v, *extra)


def _op_kernel(oa_ref, oc_ref, ub_ref, vb_ref, ws_ref, bst_ref, gb_ref, gc_ref, wout_ref,
               x_ref, g1_ref, ga1_ref, g2_ref, sc2_ref, sh2_ref, wr_ref, br_ref,
               x1_ref, h2_ref, eid_ref, ew_ref, cat):
    tm = x_ref.shape[0]
    r = lax.broadcasted_iota(I32, (CHUNK, CHUNK), 0)
    c = lax.broadcasted_iota(I32, (CHUNK, CHUNK), 1)
    lane_b = lax.broadcasted_iota(I32, (CHUNK, DB_W), 1)
    w_tril = [jnp.where(r >= c, ws_ref[g], 0.0).astype(BF16) for g in range(G_B)]
    for ch in range(tm // CHUNK):
        rows = slice(ch * CHUNK, (ch + 1) * CHUNK)
        vb = vb_ref[rows, :]
        mix = bst_ref[...]
        for g in range(G_B):
            in_group = (lane_b >= g * HD) & (lane_b < (g + 1) * HD)
            mix = mix + jnp.where(in_group, _dot(w_tril[g], vb), 0.0)
        cat[rows, SEG:SEG + DB_W] = _rms(ub_ref[rows, :] * mix, gb_ref[...]).astype(BF16)
    cat[:, 0:SEG] = oa_ref[...]
    cat[:, SEG + DB_W:] = _rms(oc_ref[...], gc_ref[...]).astype(BF16)
    m = _dot(cat[...], wout_ref[...])
    x1 = x_ref[...] + ga1_ref[0] * _rms(m, g1_ref[...])
    x1_ref[...] = x1
    h2 = _prenorm(x1, g2_ref[...], sc2_ref[0], sh2_ref[0])
    h2_ref[...] = h2
    e1, e2, w1, w2 = _route(_dot(h2.astype(BF16), wr_ref[...]) + br_ref[...])
    lane = lax.broadcasted_iota(I32, (tm, LANE), 1)
    eid_ref[...] = jnp.where(lane == 0, e1, jnp.where(lane == 1, e2, 0.0)).astype(I32)
    ew_ref[...] = jnp.where(lane == 0, w1, jnp.where(lane == 1, w2, 0.0))


def _op_call(oa, oc, ub, vb, ws, bst, gb, gc, wout, x, g1, ga1, g2, sc2, sh2, wr, br, seq):
    n = x.shape[0]
    tm = min(TM_OP, seq)
    tpb = seq // tm
    bmap = lambda i: (i // tpb, 0, 0)
    full = lambda shp: pl.BlockSpec(shp, lambda i: (0,) * len(shp))
    rows = lambda w: pl.BlockSpec((tm, w), lambda i: (i, 0))
    return pl.pallas_call(
        _op_kernel,
        grid=(n // tm,),
        in_specs=[rows(SEG), rows(SEG), rows(DB_W), rows(DB_W),
                  full((G_B, CHUNK, CHUNK)), full((CHUNK, DB_W)), full((1, DB_W)), full((1, SEG)),
                  full((D, D)), rows(D), full((1, D)), pl.BlockSpec((1, 1, D), bmap),
                  full((1, D)), pl.BlockSpec((1, 1, D), bmap), pl.BlockSpec((1, 1, D), bmap),
                  full((D, LANE)), full((1, LANE))],
        out_specs=[rows(D), rows(D), rows(LANE), rows(LANE)],
        out_shape=[jax.ShapeDtypeStruct((n, D), F32), jax.ShapeDtypeStruct((n, D), F32),
                   jax.ShapeDtypeStruct((n, LANE), I32), jax.ShapeDtypeStruct((n, LANE), F32)],
        scratch_shapes=[pltpu.VMEM((tm, D), BF16)],
        compiler_params=_cparams("parallel"),
        name="prompt_mix_out_router",
    )(oa, oc, ub, vb, ws, bst, gb, gc, wout, x, g1, ga1, g2, sc2, sh2, wr, br)


def _moe_kernel(te_ref, nv_ref, tok_ref, h2_ref, wgu_ref, wd_ref, y_ref,
                xbuf, wgu_b, wd_b, sem):
    t = pl.program_id(0)
    n_valid = nv_ref[0]
    valid = t < n_valid
    slot = lax.rem(t, 2)

    def for_rows(tile, buf_slot, act):
        def body(r, carry):
            act(pltpu.make_async_copy(h2_ref.at[pl.ds(tok_ref[tile * TM_E + r], 1)],
                                      xbuf.at[buf_slot, pl.ds(r, 1)], sem.at[buf_slot]))
            return carry
        lax.fori_loop(0, TM_E, body, 0, unroll=8)

    @pl.when(t == 0)
    def _():
        for_rows(0, 0, lambda cp: cp.start())

    @pl.when(t + 1 < n_valid)
    def _():
        for_rows(t + 1, 1 - slot, lambda cp: cp.start())

    @pl.when(valid & ((t == 0) | (te_ref[t] != te_ref[jnp.maximum(t - 1, 0)])))
    def _():
        wgu_b[...] = wgu_ref[0, 0].astype(BF16)
        wd_b[...] = wd_ref[0, 0].astype(BF16)

    @pl.when(valid)
    def _():
        for_rows(t, slot, lambda cp: cp.wait())
        hu = _dot(xbuf[slot].astype(BF16), wgu_b[...])
        hid = _silu(hu[:, :D_EXP]) * hu[:, D_EXP:]
        y_ref[...] = _dot(hid.astype(BF16), wd_b[...])

    @pl.when(jnp.logical_not(valid))
    def _():
        y_ref[...] = jnp.zeros_like(y_ref)


def _moe_call(layer, tile_e, n_valid, tok_of_slot, h2, w_gu, w_down):
    total = tok_of_slot.shape[0]
    n_tiles = total // TM_E
    expert = lambda t, te, nv, tok: (layer, te[t], 0, 0)
    return pl.pallas_call(
        _moe_kernel,
        grid_spec=pltpu.PrefetchScalarGridSpec(
            num_scalar_prefetch=3, grid=(n_tiles,),
            in_specs=[pl.BlockSpec(memory_space=pl.ANY),
                      pl.BlockSpec((1, 1, D, 2 * D_EXP), expert),
                      pl.BlockSpec((1, 1, D_EXP, D), expert)],
            out_specs=pl.BlockSpec((TM_E, D), lambda t, te, nv, tok: (t, 0)),
            scratch_shapes=[pltpu.VMEM((2, TM_E, D), F32),
                            pltpu.VMEM((D, 2 * D_EXP), BF16), pltpu.VMEM((D_EXP, D), BF16),
                            pltpu.SemaphoreType.DMA((2,))]),
        out_shape=jax.ShapeDtypeStruct((total, D), F32),
        compiler_params=_cparams("arbitrary"),
        name="moe_experts",
    )(tile_e, n_valid, tok_of_slot, h2, w_gu, w_down)


def _combine_kernel(pos_ref, y_ref, ew_ref, x1_ref, ga_ref, g3_ref, x2_ref, buf, sem, *, n_tok):
    i = pl.program_id(0)
    tm = x1_ref.shape[0]
    slot = lax.rem(i, 2)

    def for_rows(tile, buf_slot, act):
        def body(row, carry):
            for k in range(2):
                src = pos_ref[k * n_tok + tile * tm + row]
                act(pltpu.make_async_copy(y_ref.at[pl.ds(src, 1)], buf.at[buf_slot, k, pl.ds(row, 1)],
                                          sem.at[buf_slot]))
            return carry
        lax.fori_loop(0, tm, body, 0, unroll=8)

    @pl.when(i == 0)
    def _():
        for_rows(0, 0, lambda cp: cp.start())

    @pl.when(i + 1 < pl.num_programs(0))
    def _():
        for_rows(i + 1, 1 - slot, lambda cp: cp.start())

    for_rows(i, slot, lambda cp: cp.wait())
    ew = ew_ref[...]
    mo = ew[:, 0:1] * buf[slot, 0] + ew[:, 1:2] * buf[slot, 1]
    x2_ref[...] = x1_ref[...] + ga_ref[0] * _rms(mo, g3_ref[...])


def _combine_call(pos, y, ew, x1, ga2, g3, seq):
    n = x1.shape[0]
    tm = min(TM_CB, seq)
    tpb = seq // tm
    return pl.pallas_call(
        functools.partial(_combine_kernel, n_tok=n),
        grid_spec=pltpu.PrefetchScalarGridSpec(
            num_scalar_prefetch=1, grid=(n // tm,),
            in_specs=[pl.BlockSpec(memory_space=pl.ANY),
                      pl.BlockSpec((tm, LANE), lambda i, pos: (i, 0)),
                      pl.BlockSpec((tm, D), lambda i, pos: (i, 0)),
                      pl.BlockSpec((1, 1, D), lambda i, pos: (i // tpb, 0, 0)),
                      pl.BlockSpec((1, D), lambda i, pos: (0, 0))],
            out_specs=pl.BlockSpec((tm, D), lambda i, pos: (i, 0)),
            scratch_shapes=[pltpu.VMEM((2, 2, tm, D), F32), pltpu.SemaphoreType.DMA((2,))]),
        out_shape=jax.ShapeDtypeStruct((n, D), F32),
        compiler_params=_cparams("arbitrary"),
        name="moe_combine",
    )(pos, y, ew, x1, ga2, g3)


def _moe_plan(eid):
    n = eid.shape[0]
    pe = jnp.concatenate([eid[:, 0], eid[:, 1]])
    tok = jnp.concatenate([jnp.arange(n, dtype=I32)] * 2)
    onehot = (pe[:, None] == jnp.arange(N_EXP, dtype=I32)[None, :]).astype(I32)
    counts = jnp.sum(onehot, axis=0)
    rank = jnp.sum((jnp.cumsum(onehot, axis=0) - onehot) * onehot, axis=1)
    padded = ((counts + TM_E - 1) // TM_E) * TM_E
    ends = jnp.cumsum(padded)
    dest = (ends - padded)[pe] + rank
    total = 2 * n + N_EXP * TM_E
    tok_of_slot = jnp.zeros((total,), I32).at[dest].set(tok)
    n_valid = (ends[-1] // TM_E).astype(I32)
    starts = jnp.arange(total // TM_E, dtype=I32) * TM_E
    tile_e = jnp.minimum(jnp.sum((starts[:, None] >= ends[None, :]).astype(I32), axis=1), N_EXP - 1)
    tile_e = jnp.where(starts // TM_E < n_valid, tile_e, tile_e[jnp.maximum(n_valid - 1, 0)])
    return tok_of_slot, tile_e.astype(I32), n_valid.reshape(1), dest.astype(I32)


def _sk1_kernel(x_ref, g_ref, sc_ref, sh_ref, w_ref, wf_ref, z_ref, zf_ref, h_scr):
    j = pl.program_id(0)

    @pl.when(j == 0)
    def _():
        h = _prenorm(x_ref[...], g_ref[...], sc_ref[...], sh_ref[...])
        h_scr[...] = h
        zf_ref[...] = _dot3(h, wf_ref[...])

    z_ref[...] = _dot3(h_scr[...], w_ref[0])


def _sk1_call(layer, x, g0, sc, sh, w_in, wf):
    nb = x.shape[0]
    full = lambda shp: pl.BlockSpec(shp, lambda j: (0,) * len(shp))
    return pl.pallas_call(
        _sk1_kernel,
        grid=(OFF_F // TN_S,),
        in_specs=[full((nb, D)), full((1, D)), full((nb, D)), full((nb, D)),
                  pl.BlockSpec((1, D, TN_S), lambda j: (layer, 0, j)), full((D, LANE))],
        out_specs=[pl.BlockSpec((nb, TN_S), lambda j: (0, j)), full((nb, LANE))],
        out_shape=[jax.ShapeDtypeStruct((nb, OFF_F), F32), jax.ShapeDtypeStruct((nb, LANE), F32)],
        scratch_shapes=[pltpu.VMEM((nb, D), F32)],
        compiler_params=_cparams("arbitrary"),
        name="decode_in_proj",
    )(x, g0, sc, sh, w_in, wf)


def _sk2_kernel(z_ref, zf_ref, cos_ref, sa_ref, sb_ref, bf_ref, gs_ref,
                qa_ref, ka_ref, va_ref, ub_ref, vb_ref, qc_ref, kc_ref, vc_ref, lf_ref):
    c, sa, sb = cos_ref[...], sa_ref[...], sb_ref[...]
    qa_ref[...] = _rope(z_ref[:, 0:SEG], c, sa, sb) * HD ** -0.5
    ka_ref[...] = _rope(z_ref[:, SEG:2 * SEG], c, sa, sb)
    va_ref[...] = z_ref[:, 2 * SEG:OFF_B]
    ub_ref[...] = _gelu(z_ref[:, OFF_B:OFF_B + DB_W])
    vb_ref[...] = _rms(_gelu(z_ref[:, OFF_B + DB_W:OFF_C]), gs_ref[...])
    qc_ref[...] = z_ref[:, OFF_C:OFF_C + SEG] * HD ** -0.5
    kc_ref[...] = z_ref[:, OFF_C + SEG:OFF_C + 2 * SEG]
    vc_ref[...] = z_ref[:, OFF_C + 2 * SEG:OFF_F]
    lf_ref[...] = _log_sigmoid(zf_ref[...] + bf_ref[...])


def _sk2_call(z, zf, rope_tabs, bf, gs):
    nb = z.shape[0]
    s = lambda w: jax.ShapeDtypeStruct((nb, w), F32)
    return pl.pallas_call(
        _sk2_kernel,
        out_shape=[s(SEG), s(SEG), s(SEG), s(DB_W), s(DB_W), s(SEG), s(SEG), s(SEG), s(LANE)],
        name="decode_proj_post",
    )(z, zf, *rope_tabs, bf, gs)


def _lanes_sum_exact(acc):
    ones = jnp.ones((8, acc.shape[1]), BF16)
    h1 = acc.astype(BF16)
    r1 = acc - h1.astype(F32)
    h2 = r1.astype(BF16)
    h3 = (r1 - h2.astype(F32)).astype(BF16)
    y = _dot_nt(ones, h1) + _dot_nt(ones, h2) + _dot_nt(ones, h3)
    return y[0:1]


def _sk3_kernel(pt_ref, qa_ref, kan_ref, van_ref, qc_ref, kcn_ref, qct_ref, vct_ref, lfn_ref, dl_ref,
                *rest, lam_init, n_pg):
    ak, av, ck, cv, clf = (rest[i * n_pg:(i + 1) * n_pg] for i in range(5))
    (oa_ref, oc_ref, qa_s, m_a, l_a, acc_a, qc_s, m_c, l_c, acc_c, carry, s_scr, p_scr) = rest[5 * n_pg:]
    p = pl.program_id(1)
    rows_a = ROWS_A
    wide = H_A * CHUNK
    row = lax.broadcasted_iota(I32, (rows_a, 2 * HD), 0)
    lane = lax.broadcasted_iota(I32, (rows_a, 2 * HD), 1)
    pair = lax.shift_right_logical(row, 1)

    def rows_of(ref):
        out = jnp.zeros((rows_a, 2 * HD), F32)
        for h in range(H_A):
            out = jnp.where(pair == h, jnp.broadcast_to(ref[0, h:h + 1, :], (rows_a, 2 * HD)), out)
        return out

    @pl.when(p == 0)
    def _():
        second_map = lax.bitwise_and(row, 1) == 1
        q16 = jnp.where((lane < HD) != second_map, rows_of(qa_ref), 0.0)
        qa_s[...] = q16
        m_a[...] = jnp.sum(q16 * rows_of(kan_ref), axis=1, keepdims=True)
        l_a[...] = jnp.ones_like(l_a)
        acc_a[...] = rows_of(van_ref)
        m_c[...] = jnp.sum(qc_ref[0] * kcn_ref[0], axis=1, keepdims=True)
        l_c[...] = jnp.ones_like(l_c)
        lane1 = lax.broadcasted_iota(I32, (1, CHUNK), 1)
        for h in range(H_C):
            qc_s[h] = jnp.broadcast_to(qct_ref[0, :, h:h + 1], (HD, CHUNK))
            acc_c[h] = jnp.where(lane1 == 0, jnp.broadcast_to(vct_ref[0, :, h:h + 1], (HD, CHUNK)), 0.0)
        r16 = lax.broadcasted_iota(I32, (16, LANE), 0)
        l16 = lax.broadcasted_iota(I32, (16, LANE), 1)
        carry[...] = jnp.sum(jnp.where(r16 == l16, jnp.broadcast_to(lfn_ref[0], (16, LANE)), 0.0),
                             axis=1, keepdims=True)

    row_w = lax.broadcasted_iota(I32, (rows_a, wide), 0)
    col_w = lax.broadcasted_iota(I32, (rows_a, wide), 1)
    same_head = lax.shift_right_logical(col_w, 7) == lax.shift_right_logical(row_w, 1)
    qb = qa_s[...].astype(BF16)
    s_a = jnp.concatenate(
        [jnp.where(same_head, _dot_nt(qb, ak[k][0, 0].reshape(wide, 2 * HD).astype(BF16)), NEG)
         for k in range(n_pg)], axis=1)
    m_old = m_a[...]
    m_new = jnp.maximum(m_old, jnp.max(s_a, axis=1, keepdims=True))
    alpha = jnp.exp(m_old - m_new)
    pr = jnp.exp(s_a - m_new)
    l_a[...] = alpha * l_a[...] + jnp.sum(pr, axis=1, keepdims=True)
    pv = _dot(pr[:, 0:wide].astype(BF16), av[0][0, 0].reshape(wide, 2 * HD).astype(BF16))
    for k in range(1, n_pg):
        pv = pv + _dot(pr[:, k * wide:(k + 1) * wide].astype(BF16),
                       av[k][0, 0].reshape(wide, 2 * HD).astype(BF16))
    acc_a[...] = alpha * acc_a[...] + pv
    m_a[...] = m_new

    i_ = lax.broadcasted_iota(I32, (CHUNK, CHUNK), 0)
    j_ = lax.broadcasted_iota(I32, (CHUNK, CHUNK), 1)
    later_mask = (i_ > j_).astype(BF16)
    run = carry[...]
    for k in range(n_pg):
        lf = clf[k][0, 0]
        h1 = lf.astype(BF16)
        r1 = lf - h1.astype(F32)
        h2 = r1.astype(BF16)
        h3 = (r1 - h2.astype(F32)).astype(BF16)
        y = _dot(jnp.concatenate([h1, h2, h3], axis=0), later_mask)
        bias = run + (y[0:16] + y[16:32] + y[32:48])
        for h in range(H_C):
            s_scr[h:h + 1, k * CHUNK:(k + 1) * CHUNK] = (
                jnp.sum(ck[k][0, 0, h] * qc_s[h], axis=0, keepdims=True) + bias[h:h + 1, :])
        run = run + jnp.sum(lf, axis=1, keepdims=True)
    carry[...] = run
    s_c = s_scr[...]
    m_old = m_c[...]
    m_new = jnp.maximum(m_old, jnp.max(s_c, axis=1, keepdims=True))
    alpha = jnp.exp(m_old - m_new)
    pr = jnp.exp(s_c - m_new)
    l_c[...] = alpha * l_c[...] + jnp.sum(pr, axis=1, keepdims=True)
    m_c[...] = m_new
    p_scr[...] = pr
    for h in range(H_C):
        acc = alpha[h:h + 1, :] * acc_c[h]
        for k in range(n_pg):
            acc = acc + cv[k][0, 0, h] * p_scr[h:h + 1, k * CHUNK:(k + 1) * CHUNK]
        acc_c[h] = acc

    @pl.when(p == pl.num_programs(1) - 1)
    def _():
        dl = dl_ref[...]
        lam = (jnp.exp(jnp.sum(dl[0:1] * dl[1:2], axis=1, keepdims=True))
               - jnp.exp(jnp.sum(dl[2:3] * dl[3:4], axis=1, keepdims=True)) + lam_init)
        o = acc_a[...] * (1.0 / l_a[...])
        for h in range(H_A):
            oa_ref[0, h:h + 1, :] = o[2 * h:2 * h + 1] - lam * o[2 * h + 1:2 * h + 2]
        inv_l = 1.0 / l_c[...]
        for h in range(H_C):
            oc_ref[0, h:h + 1, :] = _lanes_sum_exact(acc_c[h]) * inv_l[h:h + 1, :]


def _sk3_call(layer, page_table, qa, ka, va, qc, kc, vc, lf, dl, cak, cav, cck, ccv, clf_t, lam_init):
    nb, n_pages = page_table.shape
    n_pg = math.gcd(n_pages, PAGES_PER_STEP)
    heads = lambda a, h: a.reshape(nb, h, a.shape[1] // h)
    qc3, kc3, vc3 = heads(qc, H_C), heads(kc, H_C), heads(vc, H_C)
    per_b = lambda shp: pl.BlockSpec((1,) + shp, lambda b, p, pt: (b, 0, 0))

    def page_spec(shp, k):
        def index(b, p, pt):
            return (layer, pt[b * n_pages + n_pages - 1 - (p * n_pg + k)]) + (0,) * len(shp)
        return pl.BlockSpec((1, 1) + shp, index)

    pages = ([page_spec((H_A, CHUNK, 2 * HD), k) for k in range(n_pg)] * 2
             + [page_spec((H_C, HD, CHUNK), k) for k in range(n_pg)] * 2
             + [page_spec((16, CHUNK), k) for k in range(n_pg)])
    rows_a = ROWS_A
    return pl.pallas_call(
        functools.partial(_sk3_kernel, lam_init=lam_init, n_pg=n_pg),
        grid_spec=pltpu.PrefetchScalarGridSpec(
            num_scalar_prefetch=1, grid=(nb, n_pages // n_pg),
            in_specs=[per_b((H_A, 2 * HD)), per_b((H_A, 2 * HD)), per_b((H_A, 2 * HD)),
                      per_b((H_C, HD)), per_b((H_C, HD)), per_b((HD, H_C)), per_b((HD, H_C)),
                      per_b((1, LANE)),
                      pl.BlockSpec((4, HD), lambda b, p, pt: (0, 0))] + pages,
            out_specs=[per_b((H_A, 2 * HD)), per_b((H_C, HD))],
            scratch_shapes=[pltpu.VMEM((rows_a, 2 * HD), F32), pltpu.VMEM((rows_a, 1), F32),
                            pltpu.VMEM((rows_a, 1), F32), pltpu.VMEM((rows_a, 2 * HD), F32),
                            pltpu.VMEM((H_C, HD, CHUNK), F32), pltpu.VMEM((H_C, 1), F32),
                            pltpu.VMEM((H_C, 1), F32), pltpu.VMEM((H_C, HD, CHUNK), F32),
                            pltpu.VMEM((16, 1), F32),
                            pltpu.VMEM((H_C, n_pg * CHUNK), F32), pltpu.VMEM((H_C, n_pg * CHUNK), F32)]),
        out_shape=[jax.ShapeDtypeStruct((nb, H_A, 2 * HD), F32), jax.ShapeDtypeStruct((nb, H_C, HD), F32)],
        compiler_params=_cparams("parallel", "arbitrary"),
        name="decode_paged_attention",
    )(page_table.reshape(-1), heads(qa, H_A), heads(ka, H_A), heads(va, H_A),
      qc3, kc3, jnp.swapaxes(qc3, 1, 2), jnp.swapaxes(vc3, 1, 2), lf.reshape(nb, 1, LANE), dl,
      *([cak] * n_pg), *([cav] * n_pg), *([cck] * n_pg), *([ccv] * n_pg), *([clf_t] * n_pg))


def _sk4_kernel(oa_ref, oc_ref, ub_ref, vb_ref, ws0_ref, bs0_ref, gsub_ref, gb_ref, gc_ref, w_ref,
                m_ref, cat, *, lam_init):
    j = pl.program_id(0)

    @pl.when(j == 0)
    def _():
        oa = oa_ref[...]
        parts = []
        for h in range(H_A):
            parts.append(_rms(oa[:, h * 2 * HD:(h + 1) * 2 * HD], gsub_ref[...]) * (1.0 - lam_init))
        parts.append(_rms(ub_ref[...] * (ws0_ref[...] * vb_ref[...] + bs0_ref[...]), gb_ref[...]))
        parts.append(_rms(oc_ref[...], gc_ref[...]))
        cat[...] = jnp.concatenate(parts, axis=1)

    m_ref[...] = _dot3(cat[...], w_ref[0])


def _sk4_call(layer, oa, oc, ub, vb, ws0, bs0, gsub, gb, gc, w_out, lam_init):
    nb = oa.shape[0]
    full = lambda shp: pl.BlockSpec(shp, lambda j: (0,) * len(shp))
    return pl.pallas_call(
        functools.partial(_sk4_kernel, lam_init=lam_init),
        grid=(D // TN_S,),
        in_specs=[full((nb, SEG)), full((nb, SEG)), full((nb, DB_W)), full((nb, DB_W)),
                  full((1, DB_W)), full((1, DB_W)), full((1, 2 * HD)), full((1, DB_W)), full((1, SEG)),
                  pl.BlockSpec((1, D, TN_S), lambda j: (layer, 0, j))],
        out_specs=pl.BlockSpec((nb, TN_S), lambda j: (0, j)),
        out_shape=jax.ShapeDtypeStruct((nb, D), F32),
        scratch_shapes=[pltpu.VMEM((nb, D), F32)],
        compiler_params=_cparams("arbitrary"),
        name="decode_out_proj",
    )(oa, oc, ub, vb, ws0, bs0, gsub, gb, gc, w_out)


def _sk4b_kernel(m_ref, x_ref, g1_ref, ga1_ref, g2_ref, sc2_ref, sh2_ref, wr_ref, br_ref,
                 x1_ref, h2_ref, gate_ref):
    x1 = x_ref[...] + ga1_ref[...] * _rms(m_ref[...], g1_ref[...])
    x1_ref[...] = x1
    h2 = _prenorm(x1, g2_ref[...], sc2_ref[...], sh2_ref[...])
    h2_ref[...] = h2
    e1, e2, w1, w2 = _route(_dot3(h2, wr_ref[...]) + br_ref[...])
    lane = lax.broadcasted_iota(I32, gate_ref.shape, 1).astype(F32)
    gate_ref[...] = jnp.where(lane == e1, w1, 0.0) + jnp.where(lane == e2, w2, 0.0)


def _sk4b_call(m, x, g1, ga1, g2, sc2, sh2, wr, br):
    nb = x.shape[0]
    return pl.pallas_call(
        _sk4b_kernel,
        out_shape=[jax.ShapeDtypeStruct((nb, D), F32), jax.ShapeDtypeStruct((nb, D), F32),
                   jax.ShapeDtypeStruct((nb, LANE), F32)],
        name="decode_post_router",
    )(m, x, g1, ga1, g2, sc2, sh2, wr, br)


def _sk5_kernel(el_ref, cnt_ref, h2_ref, gate_ref, x1_ref, ga_ref, g3_ref, wgu_ref, wd_ref,
                x2_ref, acc):
    s = pl.program_id(0)

    @pl.when(s == 0)
    def _():
        acc[...] = jnp.zeros_like(acc)

    @pl.when(s < cnt_ref[0])
    def _():
        hu = _dot3(h2_ref[...], wgu_ref[0, 0])
        lane = lax.broadcasted_iota(I32, gate_ref.shape, 1)
        g = jnp.sum(jnp.where(lane == el_ref[s], gate_ref[...], 0.0), axis=1, keepdims=True)
        hid = _silu(hu[:, :D_EXP]) * hu[:, D_EXP:] * g
        acc[...] += _dot3(hid, wd_ref[0, 0])

    @pl.when(s == pl.num_programs(0) - 1)
    def _():
        x2_ref[...] = x1_ref[...] + ga_ref[...] * _rms(acc[...], g3_ref[...])


def _sk5_call(layer, elist, cnt, h2, gate, x1, ga2, g3, w_gu, w_down):
    nb = h2.shape[0]
    full = lambda shp: pl.BlockSpec(shp, lambda s, el, cnt: (0,) * len(shp))
    return pl.pallas_call(
        _sk5_kernel,
        grid_spec=pltpu.PrefetchScalarGridSpec(
            num_scalar_prefetch=2, grid=(N_EXP,),
            in_specs=[full((nb, D)), full((nb, LANE)), full((nb, D)), full((nb, D)), full((1, D)),
                      pl.BlockSpec((1, 1, D, 2 * D_EXP), lambda s, el, cnt: (layer, el[s], 0, 0)),
                      pl.BlockSpec((1, 1, D_EXP, D), lambda s, el, cnt: (layer, el[s], 0, 0))],
            out_specs=full((nb, D)),
            scratch_shapes=[pltpu.VMEM((nb, D), F32)]),
        out_shape=jax.ShapeDtypeStruct((nb, D), F32),
        compiler_params=_cparams("arbitrary"),
        name="decode_moe",
    )(elist, cnt, h2, gate, x1, ga2, g3, w_gu, w_down)


def _rope_tables(pos):
    half = ROT_DIM // 2
    n = pos.shape[0]
    inv = ROPE_THETA ** (-jnp.arange(0, ROT_DIM, 2, dtype=F32) / ROT_DIM)
    ang = pos.astype(F32)[:, None] * inv[None, :]
    cos, sin = jnp.cos(ang), jnp.sin(ang)
    one = jnp.ones((n, HD - ROT_DIM), F32)
    zero = jnp.zeros((n, HD - ROT_DIM), F32)
    z8 = jnp.zeros((n, half), F32)
    c64 = jnp.concatenate([cos, cos, one], axis=1)
    sa64 = jnp.concatenate([-sin, z8, zero], axis=1)
    sb64 = jnp.concatenate([z8, sin, zero], axis=1)
    rep = lambda t: jnp.concatenate([t, t], axis=1)
    return rep(c64), rep(sa64), rep(sb64)


def _pad_lanes(a):
    return jnp.pad(a, ((0, 0), (0, LANE - a.shape[1])))


def kernel(x_prompt, x_sample, c_prompt, c_sample, cache_a_k, cache_a_v, cache_c_k, cache_c_v,
           cache_c_logf, page_table, w_ada, b_ada, g_norm, w_in, b_f, g_sgu, w_s, b_s,
           diff_lambda, g_subln, g_out_b, g_out_c, w_out, w_rg, b_rg, w_re, b_re, w_gu, w_down):
    depth = w_ada.shape[0]
    batch, seq, _ = x_prompt.shape
    nb, dec_len, _ = x_sample.shape
    assert dec_len == 1 and seq % CHUNK == 0
    n_pool, page = cache_a_k.shape[1], cache_a_k.shape[2]
    assert page == CHUNK
    past = page_table.shape[1] * page
    n = batch * seq

    rows_c = 16
    c_all = jnp.zeros((rows_c, D), F32).at[:batch].set(c_prompt).at[batch:batch + nb].set(c_sample)
    mod = _mod_call(c_all, w_ada, b_ada)

    tabs_p = _rope_tables(jnp.arange(seq))
    tabs_s = _rope_tables(jnp.full((1,), past))
    cak = jnp.transpose(cache_a_k, (0, 1, 3, 2, 4))
    cav = jnp.transpose(cache_a_v, (0, 1, 3, 2, 4))
    cck = jnp.transpose(cache_c_k, (0, 1, 3, 4, 2))
    ccv = jnp.transpose(cache_c_v, (0, 1, 3, 4, 2))
    clf_t = jnp.pad(jnp.swapaxes(cache_c_logf, 2, 3), ((0, 0), (0, 0), (0, 16 - H_C), (0, 0)))

    tq = min(TQ, seq)
    nq = seq // tq
    xp = x_prompt.reshape(n, D)
    xs = x_sample.reshape(nb, D)
    outs = [[] for _ in range(11)]
    stacked = None
    for l in range(depth):
        lam_init = 0.8 - 0.6 * math.exp(-0.3 * l)
        row = lambda v: v.reshape(1, -1)
        gn = [row(g_norm[l, k]) for k in range(4)]
        mp = [mod[l, :batch, k * D:(k + 1) * D].reshape(batch, 1, D) for k in range(6)]
        ms = [mod[l, batch:batch + nb, k * D:(k + 1) * D] for k in range(6)]
        wl = w_in[l]
        w768 = jnp.concatenate([wl[:, :OFF_B], wl[:, OFF_C:OFF_F]], axis=1).astype(BF16)
        wf = _pad_lanes(wl[:, OFF_F:])
        wb = jnp.concatenate([wl[:, OFF_B:OFF_C], wf], axis=1).astype(BF16)
        bf = _pad_lanes(row(b_f[l]))
        gs = row(g_sgu[l])
        wr = _pad_lanes(jnp.concatenate([w_rg[l], w_re[l]], axis=1))
        br = _pad_lanes(row(jnp.concatenate([b_rg[l], b_re[l]])))
        gsub, gb, gc = row(g_subln[l]), row(g_out_b[l]), row(g_out_c[l])
        dl = diff_lambda[l]

        qa, kaf, kab, vaf, vat, qc, kcf, kcb, vcf, vct = _ip1_call(
            xp, gn[0], mp[1], mp[0], w768, tabs_p, batch, seq, l, depth, stacked)
        stacked = (kaf, vaf, kcf, vcf)
        ub, vb, lfp, dcum = _ip2_call(xp, gn[0], mp[1], mp[0], wb, bf, gs, seq)
        oa = _flash_call("a", qa, kab, vat, (dl, gsub.reshape(2 * HD, 1)), batch, seq, lam_init)
        dh = (dcum[:, :H_C] * LOG2E).reshape(batch, nq, tq, H_C // 2, 2)
        dq = jnp.transpose(dh, (0, 3, 1, 4, 2))
        dk = jnp.transpose(dh, (0, 3, 1, 2, 4)).reshape(batch, H_C // 2, seq, 2)
        oc = _flash_call("c", qc, kcb, vct, (dq, dk), batch, seq)
        bst = jnp.repeat(jnp.swapaxes(b_s[l], 0, 1), HD, axis=1)
        x1, h2, eid, ew = _op_call(oa, oc, ub, vb, w_s[l], bst, gb, gc, w_out[l].astype(BF16),
                                   xp, gn[1], mp[2], gn[2], mp[4], mp[3], wr.astype(BF16), br, seq)
        tok_of_slot, tile_e, n_valid, dest = _moe_plan(eid[:, :2])
        y = _moe_call(l, tile_e, n_valid, tok_of_slot, h2, w_gu, w_down)
        xp = _combine_call(dest, y, ew, x1, mp[5], gn[3], seq)
        outs[4].append(lfp[:, :H_C])

        z, zf = _sk1_call(l, xs, gn[0], ms[1], ms[0], w_in, wf)
        qa_s, ka_s, va_s, ub_s, vb_s, qc_s, kc_s, vc_s, lf_s = _sk2_call(z, zf, tabs_s, bf, gs)
        oa_s, oc_s = _sk3_call(l, page_table, qa_s, ka_s, va_s, qc_s, kc_s, vc_s, lf_s, dl,
                               cak, cav, cck, ccv, clf_t, lam_init)
        ws0 = row(jnp.repeat(w_s[l, :, 0, 0], HD))
        bs0 = row(jnp.repeat(b_s[l, :, 0], HD))
        m_s = _sk4_call(l, oa_s.reshape(nb, SEG), oc_s.reshape(nb, SEG), ub_s, vb_s, ws0, bs0,
                        gsub, gb, gc, w_out, lam_init)
        x1_s, h2_s, gate = _sk4b_call(m_s, xs, gn[1], ms[2], gn[2], ms[4], ms[3], wr, br)
        used = jnp.any(gate[:, :N_EXP] > 0.0, axis=0)
        order = jnp.argsort(jnp.logical_not(used), stable=True).astype(I32)
        cnt = jnp.sum(used).astype(I32)
        elist = jnp.where(jnp.arange(N_EXP) < cnt, order, order[jnp.maximum(cnt - 1, 0)])
        xs = _sk5_call(l, elist, cnt.reshape(1), h2_s, gate, x1_s, ms[5], gn[3], w_gu, w_down)
        for k, a in enumerate((ka_s, va_s, kc_s, vc_s, lf_s[:, :H_C], vb_s)):
            outs[5 + k].append(a)

    st = lambda k: jnp.stack(outs[k])
    return (xp.reshape(batch, seq, D), xs.reshape(nb, 1, D),
            jnp.transpose(stacked[0], (0, 1, 3, 2, 4)), jnp.transpose(stacked[1], (0, 1, 3, 2, 4)),
            jnp.transpose(stacked[2], (0, 1, 4, 2, 3)), jnp.transpose(stacked[3], (0, 1, 4, 2, 3)),
            st(4).reshape(depth, batch, seq, H_C),
            st(5).reshape(depth, nb, 1, H_A, 2 * HD), st(6).reshape(depth, nb, 1, H_A, 2 * HD),
            st(7).reshape(depth, nb, 1, H_C, HD), st(8).reshape(depth, nb, 1, H_C, HD),
            st(9).reshape(depth, nb, 1, H_C), st(10).reshape(depth, nb, 1, G_B, HD))
```

```python
import functools
import math

import jax
import jax.numpy as jnp
from jax import lax
from jax.experimental import pallas as pl
from jax.experimental.pallas import tpu as pltpu

F32 = jnp.float32
BF16 = jnp.bfloat16
I32 = jnp.int32
U32 = jnp.uint32

D = 2048
HD = 64
H_A = 6
G_B = 8
DB_W = 512
CHUNK = 128
H_C = 12
SEG = 768
N_SEG = 6
OFF_B = 3 * SEG
OFF_C = OFF_B + 2 * DB_W
OFF_F = OFF_C + 3 * SEG
ROT_DIM = 16
ROPE_THETA = 500000.0
N_GROUPS = 4
N_EXP_PER_GROUP = 4
N_EXP = 16
D_EXP = 512
EPS = 1e-6
LANE = 128
NEG = -1e30
LOG2E = 1.4426950408889634
VMEM_LIMIT = 56 * 1024 * 1024

TM_IP = 512
TQ = 1024
TM_OP = 256
TM_E = 256
TM_CB = 256
TN_S = 512
PAGES_PER_STEP = 8
ROWS_A = 16


def _cparams(*sem):
    return pltpu.CompilerParams(dimension_semantics=sem, vmem_limit_bytes=VMEM_LIMIT)


def _dot(a, b):
    return jnp.dot(a, b, preferred_element_type=F32)


def _dot_nt(a, b):
    return lax.dot_general(a, b, (((1,), (1,)), ((), ())), preferred_element_type=F32)


def _split2(x):
    hi = x.astype(BF16)
    lo = (x - hi.astype(F32)).astype(BF16)
    return hi, lo


def _dot3(x, w):
    m = x.shape[0]
    xh, xl = _split2(x)
    wh, wl = _split2(w)
    y = _dot(jnp.concatenate([xh, xl], axis=0), wh)
    return y[:m] + y[m:] + _dot(xh, wl)


def _dot_exact_lhs(a_bf16, x):
    n = x.shape[1]
    h1 = x.astype(BF16)
    r1 = x - h1.astype(F32)
    h2 = r1.astype(BF16)
    h3 = (r1 - h2.astype(F32)).astype(BF16)
    y = _dot(a_bf16, jnp.concatenate([h1, h2, h3], axis=1))
    return y[:, :n] + y[:, n:2 * n] + y[:, 2 * n:]


def _rms(x, g):
    return x * lax.rsqrt(jnp.mean(x * x, axis=-1, keepdims=True) + EPS) * g


def _prenorm(x, g, sc, sh):
    return _rms(x, g) * (1.0 + sc) + sh


def _silu(x):
    return x * (1.0 / (1.0 + jnp.exp(-x)))


def _gelu(x):
    return x * (0.5 * (1.0 + jnp.tanh(0.7978845608028654 * (x + 0.044715 * (x * x * x)))))


def _log_sigmoid(x):
    return jnp.minimum(x, 0.0) - jnp.log1p(jnp.exp(-jnp.abs(x)))


def _rope(z, c, sa, sb):
    outs = []
    for k in range(z.shape[1] // LANE):
        zz = z[:, k * LANE:(k + 1) * LANE]
        outs.append(zz * c + pltpu.roll(zz, LANE - ROT_DIM // 2, 1) * sa
                    + pltpu.roll(zz, ROT_DIM // 2, 1) * sb)
    return jnp.concatenate(outs, axis=1)


def _route(logits):
    lane = lax.broadcasted_iota(I32, logits.shape, 1).astype(F32)
    gl = jnp.where(lane < N_GROUPS, logits, NEG)
    gmax = jnp.max(gl, axis=1, keepdims=True)
    grp = jnp.min(jnp.where(gl == gmax, lane, float(LANE)), axis=1, keepdims=True)
    p_grp = 1.0 / jnp.sum(jnp.exp(gl - gmax), axis=1, keepdims=True)
    lo = N_GROUPS + N_EXP_PER_GROUP * grp
    el = jnp.where((lane >= lo) & (lane < lo + N_EXP_PER_GROUP), logits, NEG)
    v1 = jnp.max(el, axis=1, keepdims=True)
    i1 = jnp.min(jnp.where(el == v1, lane, float(LANE)), axis=1, keepdims=True)
    el2 = jnp.where(lane == i1, NEG, el)
    v2 = jnp.max(el2, axis=1, keepdims=True)
    i2 = jnp.min(jnp.where(el2 == v2, lane, float(LANE)), axis=1, keepdims=True)
    t = jnp.exp(v2 - v1)
    w1 = p_grp * (1.0 / (1.0 + t))
    w2 = p_grp * (t / (1.0 + t))
    return i1 - N_GROUPS, i2 - N_GROUPS, w1, w2


def _mod_kernel(c_ref, w_ref, b_ref, o_ref):
    o_ref[0] = _dot3(_silu(c_ref[...]), w_ref[0]) + b_ref[0]


def _mod_call(c_all, w_ada, b_ada):
    depth, _, n6 = w_ada.shape
    rows = c_all.shape[0]
    tn = 1024
    return pl.pallas_call(
        _mod_kernel,
        grid=(depth, n6 // tn),
        in_specs=[pl.BlockSpec((rows, D), lambda l, j: (0, 0)),
                  pl.BlockSpec((1, D, tn), lambda l, j: (l, 0, j)),
                  pl.BlockSpec((1, 1, tn), lambda l, j: (l, 0, j))],
        out_specs=pl.BlockSpec((1, rows, tn), lambda l, j: (l, 0, j)),
        out_shape=jax.ShapeDtypeStruct((depth, rows, n6), F32),
        compiler_params=_cparams("parallel", "parallel"),
        name="adaln_mod",
    )(c_all, w_ada, b_ada.reshape(depth, 1, n6))


def _ip1_kernel(x_ref, g_ref, sc_ref, sh_ref, w_ref, cos_ref, sa_ref, sb_ref, *rest):
    (qa_ref, kaf_ref, kab_ref, vaf_ref, vat_ref,
     qc_ref, kcf_ref, kcb_ref, vcf_ref, vct_ref, h_scr) = rest[-11:]
    j = pl.program_id(1)
    q_scale = HD ** -0.5 * LOG2E

    @pl.when(j == 0)
    def _():
        h_scr[...] = _prenorm(x_ref[...], g_ref[...], sc_ref[0], sh_ref[0]).astype(BF16)

    z = _dot(h_scr[...], w_ref[...])

    @pl.when(j == 0)
    def _():
        qa_ref[...] = (_rope(z, cos_ref[...], sa_ref[...], sb_ref[...]) * q_scale).astype(BF16)

    @pl.when(j == 1)
    def _():
        k = _rope(z, cos_ref[...], sa_ref[...], sb_ref[...])
        for h in range(H_A):
            kaf_ref[0, 0, h] = k[:, h * 2 * HD:(h + 1) * 2 * HD]
        kab_ref[...] = k.astype(BF16)

    @pl.when(j == 2)
    def _():
        for h in range(H_A):
            vaf_ref[0, 0, h] = z[:, h * 2 * HD:(h + 1) * 2 * HD]
        vat_ref[0] = z.T.astype(BF16)

    @pl.when(j == 3)
    def _():
        qc_ref[...] = (z * q_scale).astype(BF16)

    @pl.when(j == 4)
    def _():
        zt = z.T
        for h in range(H_C):
            kcf_ref[0, 0, h] = zt[h * HD:(h + 1) * HD, :]
        kcb_ref[...] = z.astype(BF16)

    @pl.when(j == 5)
    def _():
        zt = z.T
        for h in range(H_C):
            vcf_ref[0, 0, h] = zt[h * HD:(h + 1) * HD, :]
        vct_ref[0] = zt.astype(BF16)


def _ip1_call(x, g0, sc, sh, w768, rope_tabs, batch, seq, layer, depth, stacked):
    n = x.shape[0]
    tm = min(TM_IP, seq)
    tpb = seq // tm
    bmap = lambda i, j: (i // tpb, 0, 0)
    row = pl.BlockSpec((tm, SEG), lambda i, j: (i, 0))
    tab = pl.BlockSpec((tm, LANE), lambda i, j: (i % tpb, 0))
    a_nat = pl.BlockSpec((1, 1, H_A, tm, 2 * HD), lambda i, j: (layer, i // tpb, 0, i % tpb, 0))
    c_nat = pl.BlockSpec((1, 1, H_C, HD, tm), lambda i, j: (layer, i // tpb, 0, 0, i % tpb))
    v_t = pl.BlockSpec((1, SEG, tm), lambda i, j: (i, 0, 0))
    b = jax.ShapeDtypeStruct((n, SEG), BF16)
    fa = jax.ShapeDtypeStruct((depth, batch, H_A, seq, 2 * HD), F32)
    fc = jax.ShapeDtypeStruct((depth, batch, H_C, HD, seq), F32)
    stacked = () if stacked is None else tuple(stacked)
    n_in = 8
    bt = jax.ShapeDtypeStruct((n // tm, SEG, tm), BF16)
    return pl.pallas_call(
        _ip1_kernel,
        grid=(n // tm, N_SEG),
        in_specs=[pl.BlockSpec((tm, D), lambda i, j: (i, 0)),
                  pl.BlockSpec((1, D), lambda i, j: (0, 0)),
                  pl.BlockSpec((1, 1, D), bmap), pl.BlockSpec((1, 1, D), bmap),
                  pl.BlockSpec((D, SEG), lambda i, j: (0, j)),
                  tab, tab, tab] + [pl.BlockSpec(memory_space=pl.ANY)] * len(stacked),
        out_specs=[row, a_nat, row, a_nat, v_t, row, c_nat, row, c_nat, v_t],
        out_shape=[b, fa, b, fa, bt, b, fc, b, fc, bt],
        input_output_aliases={n_in + k: o for k, o in enumerate((1, 3, 6, 8)[:len(stacked)])},
        scratch_shapes=[pltpu.VMEM((tm, D), BF16)],
        compiler_params=_cparams("parallel", "arbitrary"),
        name="prompt_in_proj",
    )(x, g0, sc, sh, w768, *rope_tabs, *stacked)


def _ip2_kernel(x_ref, g_ref, sc_ref, sh_ref, w_ref, bf_ref, gs_ref,
                ub_ref, vb_ref, lf_ref, d_ref, carry, *, tpb):
    i = pl.program_id(0)
    tm = x_ref.shape[0]
    h = _prenorm(x_ref[...], g_ref[...], sc_ref[0], sh_ref[0]).astype(BF16)
    z = _dot(h, w_ref[...])
    ub_ref[...] = _gelu(z[:, :DB_W])
    vb_ref[...] = _rms(_gelu(z[:, DB_W:2 * DB_W]), gs_ref[...]).astype(BF16)
    lf = _log_sigmoid(z[:, 2 * DB_W:] + bf_ref[...])
    lf_ref[...] = lf

    @pl.when(i % tpb == 0)
    def _():
        carry[...] = jnp.zeros_like(carry)

    r = lax.broadcasted_iota(I32, (tm, tm), 0)
    c = lax.broadcasted_iota(I32, (tm, tm), 1)
    d = _dot_exact_lhs((r >= c).astype(BF16), lf) + carry[...]
    d_ref[...] = d
    carry[...] = d[tm - 1:tm, :]


def _ip2_call(x, g0, sc, sh, wb, bf, gs, seq):
    n = x.shape[0]
    tm = min(TM_IP, seq)
    tpb = seq // tm
    bmap = lambda i: (i // tpb, 0, 0)
    full = lambda shp: pl.BlockSpec(shp, lambda i: (0,) * len(shp))
    return pl.pallas_call(
        functools.partial(_ip2_kernel, tpb=tpb),
        grid=(n // tm,),
        in_specs=[pl.BlockSpec((tm, D), lambda i: (i, 0)), full((1, D)),
                  pl.BlockSpec((1, 1, D), bmap), pl.BlockSpec((1, 1, D), bmap),
                  full((D, 2 * DB_W + LANE)), full((1, LANE)), full((1, DB_W))],
        out_specs=[pl.BlockSpec((tm, DB_W), lambda i: (i, 0)),
                   pl.BlockSpec((tm, DB_W), lambda i: (i, 0)),
                   pl.BlockSpec((tm, LANE), lambda i: (i, 0)),
                   pl.BlockSpec((tm, LANE), lambda i: (i, 0))],
        out_shape=[jax.ShapeDtypeStruct((n, DB_W), F32), jax.ShapeDtypeStruct((n, DB_W), BF16),
                   jax.ShapeDtypeStruct((n, LANE), F32), jax.ShapeDtypeStruct((n, LANE), F32)],
        scratch_shapes=[pltpu.VMEM((1, LANE), F32)],
        compiler_params=_cparams("arbitrary"),
        name="prompt_sgu_gate_proj",
    )(x, g0, sc, sh, wb, bf, gs)


def _flash_kernel(*refs, mode, tq, tv, lam_init):
    if mode == "a":
        q_ref, k_ref, vt_ref, dl_ref, gs_ref, o_ref, qq, m_s, acc = refs
    else:
        q_ref, k_ref, vt_ref, dq_ref, dk_ref, o_ref, qq, m_s, acc = refs
    qi = pl.program_id(2)
    q = q_ref[...]
    lane = lax.broadcasted_iota(I32, q.shape, 1)
    qq[0:tq, :] = jnp.where(lane < HD, q, jnp.zeros_like(q))
    qq[tq:2 * tq, :] = jnp.where(lane >= HD, q, jnp.zeros_like(q))
    m_s[...] = jnp.full_like(m_s, NEG)
    acc[...] = jnp.zeros_like(acc)
    ones = jnp.ones((16, tv), BF16)
    if mode == "c":
        dq = dq_ref[0, 0, 0]
        dq_row = jnp.concatenate([dq[0:1, :], dq[1:2, :]], axis=1)

    full = ((0, 2 * tq),)
    half = tq // 2
    split_diag = half % tv == 0

    def take(ref, cols):
        if cols == full:
            return ref[...]
        return jnp.concatenate([ref[:, a:a + w] for a, w in cols], axis=1)

    def scores(kb, k0, kn, cols):
        start = pl.multiple_of(kb * tq + k0, tv)
        k = k_ref[pl.ds(start, kn), :]
        if cols == full:
            q_rows = qq[...]
        else:
            q_rows = jnp.concatenate([qq[a:a + w, :] for a, w in cols], axis=0)
        s = _dot_nt(k, q_rows)
        if mode == "c":
            dk = dk_ref[0, 0, pl.ds(start, kn), :]
            ranges = ((0, tq), (tq, tq)) if cols == full else cols
            dk_cols = jnp.concatenate(
                [jnp.broadcast_to(dk[:, a // tq:a // tq + 1], (kn, w)) for a, w in ranges], axis=1)
            s = s + (take_row(dq_row, cols) - dk_cols)
        return s

    def take_row(row, cols):
        if cols == full:
            return row
        return jnp.concatenate([row[:, a:a + w] for a, w in cols], axis=1)

    def causal(s, kn, width):
        r = lax.broadcasted_iota(I32, (kn, width), 0)
        c = lax.broadcasted_iota(I32, (kn, width), 1)
        return jnp.where(jnp.concatenate([r <= c, r <= c], axis=1), s, NEG)

    def update(s, kb, k0, cols):
        m_old = take(m_s, cols)
        m_new = jnp.maximum(m_old, jnp.max(s, axis=0, keepdims=True))
        alpha = jnp.exp2(m_old - m_new)
        pb = jnp.exp2(s - m_new).astype(BF16)
        pv = None
        for j in range(s.shape[0] // tv):
            blk = kb * (tq // tv) + (k0 // tv + j)
            t = _dot(jnp.concatenate([vt_ref[blk], ones], axis=0), pb[j * tv:(j + 1) * tv])
            pv = t if pv is None else pv + t
        new_acc = alpha * take(acc, cols) + pv
        off = 0
        for a, w in cols:
            acc[:, a:a + w] = new_acc[:, off:off + w]
            m_s[:, a:a + w] = m_new[:, off:off + w]
            off += w

    def body(kb, carry):
        update(scores(kb, 0, tq, full), kb, 0, full)
        return carry

    lax.fori_loop(0, qi, body, 0)
    if split_diag:
        late = ((half, half), (tq + half, half))
        update(causal(scores(qi, 0, half, full), half, tq), qi, 0, full)
        update(causal(scores(qi, half, half, late), half, half), qi, half, late)
    else:
        update(causal(scores(qi, 0, tq, full), tq, tq), qi, 0, full)

    o = acc[0:LANE, :] * (1.0 / acc[LANE:LANE + 1, :])
    if mode == "a":
        dl = dl_ref[...]
        lam = (jnp.exp(jnp.sum(dl[0:1] * dl[1:2], axis=1, keepdims=True))
               - jnp.exp(jnp.sum(dl[2:3] * dl[3:4], axis=1, keepdims=True)) + lam_init)
        oa = o[:, 0:tq] - lam * o[:, tq:2 * tq]
        oa = oa * lax.rsqrt(jnp.mean(oa * oa, axis=0, keepdims=True) + EPS) * gs_ref[...]
        o_ref[...] = (oa * (1.0 - lam_init)).T.astype(o_ref.dtype)
    else:
        d_row = lax.broadcasted_iota(I32, (2 * HD, tq), 0)
        o_ref[...] = jnp.where(d_row < HD, o[:, 0:tq], o[:, tq:2 * tq]).T


def _flash_call(mode, q, k, v, extra, batch, seq, lam_init=0.0):
    n = q.shape[0]
    tq = min(TQ, seq)
    tv = min(TM_IP, seq)
    nq = seq // tq
    qspec = pl.BlockSpec((tq, LANE), lambda b, g, i: (b * nq + i, g))
    kspec = pl.BlockSpec((seq, LANE), lambda b, g, i: (b, g))
    vtspec = pl.BlockSpec((seq // tv, LANE, tv), lambda b, g, i: (b, g, 0))
    if mode == "a":
        especs = [pl.BlockSpec((4, HD), lambda b, g, i: (0, 0)),
                  pl.BlockSpec((LANE, 1), lambda b, g, i: (0, 0))]
        out_dtype = BF16
    else:
        especs = [pl.BlockSpec((1, 1, 1, 2, tq), lambda b, g, i: (b, g, i, 0, 0)),
                  pl.BlockSpec((1, 1, seq, 2), lambda b, g, i: (b, g, 0, 0))]
        out_dtype = F32
    return pl.pallas_call(
        functools.partial(_flash_kernel, mode=mode, tq=tq, tv=tv, lam_init=lam_init),
        grid=(batch, SEG // LANE, nq),
        in_specs=[qspec, kspec, vtspec] + especs,
        out_specs=qspec,
        out_shape=jax.ShapeDtypeStruct((n, SEG), out_dtype),
        scratch_shapes=[pltpu.VMEM((2 * tq, LANE), BF16), pltpu.VMEM((1, 2 * tq), F32),
                        pltpu.VMEM((LANE + 16, 2 * tq), F32)],
        compiler_params=_cparams("parallel", "parallel", "parallel"),
        name="prompt_flash_" + mode,
    )(q, k, v, *extra)


def _op_kernel(oa_ref, oc_ref, ub_ref, vb_ref, ws_ref, bst_ref, gb_ref, gc_ref, wout_ref,
               x_ref, g1_ref, ga1_ref, g2_ref, sc2_ref, sh2_ref, wr_ref, br_ref,
               x1_ref, h2_ref, eid_ref, ew_ref, cat):
    tm = x_ref.shape[0]
    r = lax.broadcasted_iota(I32, (CHUNK, CHUNK), 0)
    c = lax.broadcasted_iota(I32, (CHUNK, CHUNK), 1)
    lane_b = lax.broadcasted_iota(I32, (CHUNK, DB_W), 1)
    w_tril = [jnp.where(r >= c, ws_ref[g], 0.0).astype(BF16) for g in range(G_B)]
    for ch in range(tm // CHUNK):
        rows = slice(ch * CHUNK, (ch + 1) * CHUNK)
        vb = vb_ref[rows, :]
        mix = bst_ref[...]
        for g in range(G_B):
            in_group = (lane_b >= g * HD) & (lane_b < (g + 1) * HD)
            mix = mix + jnp.where(in_group, _dot(w_tril[g], vb), 0.0)
        cat[rows, SEG:SEG + DB_W] = _rms(ub_ref[rows, :] * mix, gb_ref[...]).astype(BF16)
    cat[:, 0:SEG] = oa_ref[...]
    cat[:, SEG + DB_W:] = _rms(oc_ref[...], gc_ref[...]).astype(BF16)
    m = _dot(cat[...], wout_ref[...])
    x1 = x_ref[...] + ga1_ref[0] * _rms(m, g1_ref[...])
    x1_ref[...] = x1
    h2 = _prenorm(x1, g2_ref[...], sc2_ref[0], sh2_ref[0]).astype(BF16)
    lo = lax.bitcast_convert_type(h2[:, :D // 2].astype(F32), U32)
    hi = lax.bitcast_convert_type(h2[:, D // 2:].astype(F32), U32)
    h2_ref[...] = (hi & jnp.uint32(0xFFFF0000)) | (lo >> 16)
    e1, e2, w1, w2 = _route(_dot(h2, wr_ref[...]) + br_ref[...])
    lane = lax.broadcasted_iota(I32, (tm, LANE), 1)
    eid_ref[...] = jnp.where(lane == 0, e1, jnp.where(lane == 1, e2, 0.0)).astype(I32)
    ew_ref[...] = jnp.where(lane == 0, w1, jnp.where(lane == 1, w2, 0.0))


def _op_call(oa, oc, ub, vb, ws, bst, gb, gc, wout, x, g1, ga1, g2, sc2, sh2, wr, br, seq):
    n = x.shape[0]
    tm = min(TM_OP, seq)
    tpb = seq // tm
    bmap = lambda i: (i // tpb, 0, 0)
    full = lambda shp: pl.BlockSpec(shp, lambda i: (0,) * len(shp))
    rows = lambda w: pl.BlockSpec((tm, w), lambda i: (i, 0))
    return pl.pallas_call(
        _op_kernel,
        grid=(n // tm,),
        in_specs=[rows(SEG), rows(SEG), rows(DB_W), rows(DB_W),
                  full((G_B, CHUNK, CHUNK)), full((CHUNK, DB_W)), full((1, DB_W)), full((1, SEG)),
                  full((D, D)), rows(D), full((1, D)), pl.BlockSpec((1, 1, D), bmap),
                  full((1, D)), pl.BlockSpec((1, 1, D), bmap), pl.BlockSpec((1, 1, D), bmap),
                  full((D, LANE)), full((1, LANE))],
        out_specs=[rows(D), rows(D // 2), rows(LANE), rows(LANE)],
        out_shape=[jax.ShapeDtypeStruct((n, D), F32), jax.ShapeDtypeStruct((n, D // 2), U32),
                   jax.ShapeDtypeStruct((n, LANE), I32), jax.ShapeDtypeStruct((n, LANE), F32)],
        scratch_shapes=[pltpu.VMEM((tm, D), BF16)],
        compiler_params=_cparams("parallel"),
        name="prompt_mix_out_router",
    )(oa, oc, ub, vb, ws, bst, gb, gc, wout, x, g1, ga1, g2, sc2, sh2, wr, br)


def _moe_kernel(te_ref, nv_ref, tok_ref, h2_ref, wgu_ref, wd_ref, y_ref,
                xbuf, wgu_b, wd_b, sem):
    t = pl.program_id(0)
    n_valid = nv_ref[0]
    valid = t < n_valid
    slot = lax.rem(t, 2)

    def for_rows(tile, buf_slot, act):
        def body(r, carry):
            act(pltpu.make_async_copy(h2_ref.at[pl.ds(tok_ref[tile * TM_E + r], 1)],
                                      xbuf.at[buf_slot, pl.ds(r, 1)], sem.at[buf_slot]))
            return carry
        lax.fori_loop(0, TM_E, body, 0, unroll=8)

    @pl.when(t == 0)
    def _():
        for_rows(0, 0, lambda cp: cp.start(priority=1))

    @pl.when(t + 1 < n_valid)
    def _():
        for_rows(t + 1, 1 - slot, lambda cp: cp.start(priority=1))

    @pl.when(valid & ((t == 0) | (te_ref[t] != te_ref[jnp.maximum(t - 1, 0)])))
    def _():
        wgu_b[...] = wgu_ref[0, 0].astype(BF16)
        wd_b[...] = wd_ref[0, 0].astype(BF16)

    @pl.when(valid)
    def _():
        for_rows(t, slot, lambda cp: cp.wait())
        u = xbuf[slot]
        x = jnp.concatenate([lax.bitcast_convert_type(u << 16, F32).astype(BF16),
                             lax.bitcast_convert_type(u & jnp.uint32(0xFFFF0000), F32).astype(BF16)], axis=1)
        hu = _dot(x, wgu_b[...])
        hid = _silu(hu[:, :D_EXP]) * hu[:, D_EXP:]
        y_ref[...] = _dot(hid.astype(BF16), wd_b[...])

    @pl.when(jnp.logical_not(valid))
    def _():
        y_ref[...] = jnp.zeros_like(y_ref)


def _moe_call(layer, tile_e, n_valid, tok_of_slot, h2, w_gu, w_down):
    total = tok_of_slot.shape[0]
    n_tiles = total // TM_E
    expert = lambda t, te, nv, tok: (layer, te[t], 0, 0)
    return pl.pallas_call(
        _moe_kernel,
        grid_spec=pltpu.PrefetchScalarGridSpec(
            num_scalar_prefetch=3, grid=(n_tiles,),
            in_specs=[pl.BlockSpec(memory_space=pl.ANY),
                      pl.BlockSpec((1, 1, D, 2 * D_EXP), expert),
                      pl.BlockSpec((1, 1, D_EXP, D), expert)],
            out_specs=pl.BlockSpec((TM_E, D), lambda t, te, nv, tok: (t, 0)),
            scratch_shapes=[pltpu.VMEM((2, TM_E, D // 2), U32),
                            pltpu.VMEM((D, 2 * D_EXP), BF16), pltpu.VMEM((D_EXP, D), BF16),
                            pltpu.SemaphoreType.DMA((2,))]),
        out_shape=jax.ShapeDtypeStruct((total, D), F32),
        compiler_params=_cparams("arbitrary"),
        name="moe_experts",
    )(tile_e, n_valid, tok_of_slot, h2, w_gu, w_down)


def _combine_kernel(pos_ref, y_ref, ew_ref, x1_ref, ga_ref, g3_ref, x2_ref, buf, sem, *, n_tok):
    i = pl.program_id(0)
    tm = x1_ref.shape[0]
    slot = lax.rem(i, 2)

    def for_rows(tile, buf_slot, act):
        def body(row, carry):
            for k in range(2):
                src = pos_ref[k * n_tok + tile * tm + row]
                act(pltpu.make_async_copy(y_ref.at[pl.ds(src, 1)], buf.at[buf_slot, k, pl.ds(row, 1)],
                                          sem.at[buf_slot]), k)
            return carry
        lax.fori_loop(0, tm, body, 0, unroll=8)

    @pl.when(i == 0)
    def _():
        for_rows(0, 0, lambda cp, k: cp.start(priority=k))

    @pl.when(i + 1 < pl.num_programs(0))
    def _():
        for_rows(i + 1, 1 - slot, lambda cp, k: cp.start(priority=k))

    for_rows(i, slot, lambda cp, k: cp.wait())
    ew = ew_ref[...]
    mo = ew[:, 0:1] * buf[slot, 0] + ew[:, 1:2] * buf[slot, 1]
    x2_ref[...] = x1_ref[...] + ga_ref[0] * _rms(mo, g3_ref[...])


def _combine_call(pos, y, ew, x1, ga2, g3, seq):
    n = x1.shape[0]
    tm = min(TM_CB, seq)
    tpb = seq // tm
    return pl.pallas_call(
        functools.partial(_combine_kernel, n_tok=n),
        grid_spec=pltpu.PrefetchScalarGridSpec(
            num_scalar_prefetch=1, grid=(n // tm,),
            in_specs=[pl.BlockSpec(memory_space=pl.ANY),
                      pl.BlockSpec((tm, LANE), lambda i, pos: (i, 0)),
                      pl.BlockSpec((tm, D), lambda i, pos: (i, 0)),
                      pl.BlockSpec((1, 1, D), lambda i, pos: (i // tpb, 0, 0)),
                      pl.BlockSpec((1, D), lambda i, pos: (0, 0))],
            out_specs=pl.BlockSpec((tm, D), lambda i, pos: (i, 0)),
            scratch_shapes=[pltpu.VMEM((2, 2, tm, D), F32), pltpu.SemaphoreType.DMA((2,))]),
        out_shape=jax.ShapeDtypeStruct((n, D), F32),
        compiler_params=_cparams("arbitrary"),
        name="moe_combine",
    )(pos, y, ew, x1, ga2, g3)


def _moe_plan(eid):
    n = eid.shape[0]
    pe = jnp.concatenate([eid[:, 0], eid[:, 1]])
    tok = jnp.concatenate([jnp.arange(n, dtype=I32)] * 2)
    onehot = (pe[:, None] == jnp.arange(N_EXP, dtype=I32)[None, :]).astype(I32)
    counts = jnp.sum(onehot, axis=0)
    rank = jnp.sum((jnp.cumsum(onehot, axis=0) - onehot) * onehot, axis=1)
    padded = ((counts + TM_E - 1) // TM_E) * TM_E
    ends = jnp.cumsum(padded)
    dest = (ends - padded)[pe] + rank
    total = 2 * n + N_EXP * TM_E
    tok_of_slot = jnp.zeros((total,), I32).at[dest].set(tok)
    n_valid = (ends[-1] // TM_E).astype(I32)
    starts = jnp.arange(total // TM_E, dtype=I32) * TM_E
    tile_e = jnp.minimum(jnp.sum((starts[:, None] >= ends[None, :]).astype(I32), axis=1), N_EXP - 1)
    tile_e = jnp.where(starts // TM_E < n_valid, tile_e, tile_e[jnp.maximum(n_valid - 1, 0)])
    return tok_of_slot, tile_e.astype(I32), n_valid.reshape(1), dest.astype(I32)


def _sk1_kernel(x_ref, g_ref, sc_ref, sh_ref, w_ref, wf_ref, z_ref, zf_ref, h_scr):
    j = pl.program_id(0)

    @pl.when(j == 0)
    def _():
        h = _prenorm(x_ref[...], g_ref[...], sc_ref[...], sh_ref[...])
        h_scr[...] = h
        zf_ref[...] = _dot3(h, wf_ref[...])

    z_ref[...] = _dot3(h_scr[...], w_ref[0])


def _sk1_call(layer, x, g0, sc, sh, w_in, wf):
    nb = x.shape[0]
    full = lambda shp: pl.BlockSpec(shp, lambda j: (0,) * len(shp))
    return pl.pallas_call(
        _sk1_kernel,
        grid=(OFF_F // TN_S,),
        in_specs=[full((nb, D)), full((1, D)), full((nb, D)), full((nb, D)),
                  pl.BlockSpec((1, D, TN_S), lambda j: (layer, 0, j)), full((D, LANE))],
        out_specs=[pl.BlockSpec((nb, TN_S), lambda j: (0, j)), full((nb, LANE))],
        out_shape=[jax.ShapeDtypeStruct((nb, OFF_F), F32), jax.ShapeDtypeStruct((nb, LANE), F32)],
        scratch_shapes=[pltpu.VMEM((nb, D), F32)],
        compiler_params=_cparams("arbitrary"),
        name="decode_in_proj",
    )(x, g0, sc, sh, w_in, wf)


def _sk2_kernel(z_ref, zf_ref, cos_ref, sa_ref, sb_ref, bf_ref, gs_ref,
                qa_ref, ka_ref, va_ref, ub_ref, vb_ref, qc_ref, kc_ref, vc_ref, lf_ref):
    c, sa, sb = cos_ref[...], sa_ref[...], sb_ref[...]
    qa_ref[...] = _rope(z_ref[:, 0:SEG], c, sa, sb) * HD ** -0.5
    ka_ref[...] = _rope(z_ref[:, SEG:2 * SEG], c, sa, sb)
    va_ref[...] = z_ref[:, 2 * SEG:OFF_B]
    ub_ref[...] = _gelu(z_ref[:, OFF_B:OFF_B + DB_W])
    vb_ref[...] = _rms(_gelu(z_ref[:, OFF_B + DB_W:OFF_C]), gs_ref[...])
    qc_ref[...] = z_ref[:, OFF_C:OFF_C + SEG] * HD ** -0.5
    kc_ref[...] = z_ref[:, OFF_C + SEG:OFF_C + 2 * SEG]
    vc_ref[...] = z_ref[:, OFF_C + 2 * SEG:OFF_F]
    lf_ref[...] = _log_sigmoid(zf_ref[...] + bf_ref[...])


def _sk2_call(z, zf, rope_tabs, bf, gs):
    nb = z.shape[0]
    s = lambda w: jax.ShapeDtypeStruct((nb, w), F32)
    return pl.pallas_call(
        _sk2_kernel,
        out_shape=[s(SEG), s(SEG), s(SEG), s(DB_W), s(DB_W), s(SEG), s(SEG), s(SEG), s(LANE)],
        name="decode_proj_post",
    )(z, zf, *rope_tabs, bf, gs)


def _lanes_sum_exact(acc):
    ones = jnp.ones((8, acc.shape[1]), BF16)
    h1 = acc.astype(BF16)
    r1 = acc - h1.astype(F32)
    h2 = r1.astype(BF16)
    h3 = (r1 - h2.astype(F32)).astype(BF16)
    y = _dot_nt(ones, h1) + _dot_nt(ones, h2) + _dot_nt(ones, h3)
    return y[0:1]


def _sk3_kernel(pt_ref, qa_ref, kan_ref, van_ref, qc_ref, kcn_ref, qct_ref, vct_ref, lfn_ref, dl_ref,
                *rest, lam_init, n_pg):
    ak, av, ck, cv, clf = (rest[i * n_pg:(i + 1) * n_pg] for i in range(5))
    (oa_ref, oc_ref, qa_s, m_a, l_a, acc_a, qc_s, m_c, l_c, acc_c, carry, s_scr, p_scr) = rest[5 * n_pg:]
    p = pl.program_id(1)
    rows_a = ROWS_A
    wide = H_A * CHUNK
    row = lax.broadcasted_iota(I32, (rows_a, 2 * HD), 0)
    lane = lax.broadcasted_iota(I32, (rows_a, 2 * HD), 1)
    pair = lax.shift_right_logical(row, 1)

    def rows_of(ref):
        out = jnp.zeros((rows_a, 2 * HD), F32)
        for h in range(H_A):
            out = jnp.where(pair == h, jnp.broadcast_to(ref[0, h:h + 1, :], (rows_a, 2 * HD)), out)
        return out

    @pl.when(p == 0)
    def _():
        second_map = lax.bitwise_and(row, 1) == 1
        q16 = jnp.where((lane < HD) != second_map, rows_of(qa_ref), 0.0)
        qa_s[...] = q16
        m_a[...] = jnp.sum(q16 * rows_of(kan_ref), axis=1, keepdims=True)
        l_a[...] = jnp.ones_like(l_a)
        acc_a[...] = rows_of(van_ref)
        m_c[...] = jnp.sum(qc_ref[0] * kcn_ref[0], axis=1, keepdims=True)
        l_c[...] = jnp.ones_like(l_c)
        lane1 = lax.broadcasted_iota(I32, (1, CHUNK), 1)
        for h in range(H_C):
            qc_s[h] = jnp.broadcast_to(qct_ref[0, :, h:h + 1], (HD, CHUNK))
            acc_c[h] = jnp.where(lane1 == 0, jnp.broadcast_to(vct_ref[0, :, h:h + 1], (HD, CHUNK)), 0.0)
        r16 = lax.broadcasted_iota(I32, (16, LANE), 0)
        l16 = lax.broadcasted_iota(I32, (16, LANE), 1)
        carry[...] = jnp.sum(jnp.where(r16 == l16, jnp.broadcast_to(lfn_ref[0], (16, LANE)), 0.0),
                             axis=1, keepdims=True)

    row_w = lax.broadcasted_iota(I32, (rows_a, wide), 0)
    col_w = lax.broadcasted_iota(I32, (rows_a, wide), 1)
    same_head = lax.shift_right_logical(col_w, 7) == lax.shift_right_logical(row_w, 1)
    qb = qa_s[...].astype(BF16)
    s_a = jnp.concatenate(
        [jnp.where(same_head, _dot_nt(qb, ak[k][0, 0].reshape(wide, 2 * HD).astype(BF16)), NEG)
         for k in range(n_pg)], axis=1)
    m_old = m_a[...]
    m_new = jnp.maximum(m_old, jnp.max(s_a, axis=1, keepdims=True))
    alpha = jnp.exp(m_old - m_new)
    pr = jnp.exp(s_a - m_new)
    l_a[...] = alpha * l_a[...] + jnp.sum(pr, axis=1, keepdims=True)
    pv = _dot(pr[:, 0:wide].astype(BF16), av[0][0, 0].reshape(wide, 2 * HD).astype(BF16))
    for k in range(1, n_pg):
        pv = pv + _dot(pr[:, k * wide:(k + 1) * wide].astype(BF16),
                       av[k][0, 0].reshape(wide, 2 * HD).astype(BF16))
    acc_a[...] = alpha * acc_a[...] + pv
    m_a[...] = m_new

    i_ = lax.broadcasted_iota(I32, (CHUNK, CHUNK), 0)
    j_ = lax.broadcasted_iota(I32, (CHUNK, CHUNK), 1)
    later_mask = (i_ > j_).astype(BF16)
    run = carry[...]
    for k in range(n_pg):
        lf = clf[k][0, 0]
        h1 = lf.astype(BF16)
        r1 = lf - h1.astype(F32)
        h2 = r1.astype(BF16)
        h3 = (r1 - h2.astype(F32)).astype(BF16)
        y = _dot(jnp.concatenate([h1, h2, h3], axis=0), later_mask)
        bias = run + (y[0:16] + y[16:32] + y[32:48])
        for h in range(H_C):
            s_scr[h:h + 1, k * CHUNK:(k + 1) * CHUNK] = (
                jnp.sum(ck[k][0, 0, h] * qc_s[h], axis=0, keepdims=True) + bias[h:h + 1, :])
        run = run + jnp.sum(lf, axis=1, keepdims=True)
    carry[...] = run
    s_c = s_scr[...]
    m_old = m_c[...]
    m_new = jnp.maximum(m_old, jnp.max(s_c, axis=1, keepdims=True))
    alpha = jnp.exp(m_old - m_new)
    pr = jnp.exp(s_c - m_new)
    l_c[...] = alpha * l_c[...] + jnp.sum(pr, axis=1, keepdims=True)
    m_c[...] = m_new
    p_scr[...] = pr
    for h in range(H_C):
        acc = alpha[h:h + 1, :] * acc_c[h]
        for k in range(n_pg):
            acc = acc + cv[k][0, 0, h] * p_scr[h:h + 1, k * CHUNK:(k + 1) * CHUNK]
        acc_c[h] = acc

    @pl.when(p == pl.num_programs(1) - 1)
    def _():
        dl = dl_ref[...]
        lam = (jnp.exp(jnp.sum(dl[0:1] * dl[1:2], axis=1, keepdims=True))
               - jnp.exp(jnp.sum(dl[2:3] * dl[3:4], axis=1, keepdims=True)) + lam_init)
        o = acc_a[...] * (1.0 / l_a[...])
        for h in range(H_A):
            oa_ref[0, h:h + 1, :] = o[2 * h:2 * h + 1] - lam * o[2 * h + 1:2 * h + 2]
        inv_l = 1.0 / l_c[...]
        for h in range(H_C):
            oc_ref[0, h:h + 1, :] = _lanes_sum_exact(acc_c[h]) * inv_l[h:h + 1, :]


def _sk3_call(layer, page_table, qa, ka, va, qc, kc, vc, lf, dl, cak, cav, cck, ccv, clf_t, lam_init):
    nb, n_pages = page_table.shape
    n_pg = math.gcd(n_pages, PAGES_PER_STEP)
    heads = lambda a, h: a.reshape(nb, h, a.shape[1] // h)
    qc3, kc3, vc3 = heads(qc, H_C), heads(kc, H_C), heads(vc, H_C)
    per_b = lambda shp: pl.BlockSpec((1,) + shp, lambda b, p, pt: (b, 0, 0))

    def page_spec(shp, k):
        def index(b, p, pt):
            return (layer, pt[b * n_pages + n_pages - 1 - (p * n_pg + k)]) + (0,) * len(shp)
        return pl.BlockSpec((1, 1) + shp, index)

    pages = ([page_spec((H_A, CHUNK, 2 * HD), k) for k in range(n_pg)] * 2
             + [page_spec((H_C, HD, CHUNK), k) for k in range(n_pg)] * 2
             + [page_spec((16, CHUNK), k) for k in range(n_pg)])
    rows_a = ROWS_A
    return pl.pallas_call(
        functools.partial(_sk3_kernel, lam_init=lam_init, n_pg=n_pg),
        grid_spec=pltpu.PrefetchScalarGridSpec(
            num_scalar_prefetch=1, grid=(nb, n_pages // n_pg),
            in_specs=[per_b((H_A, 2 * HD)), per_b((H_A, 2 * HD)), per_b((H_A, 2 * HD)),
                      per_b((H_C, HD)), per_b((H_C, HD)), per_b((HD, H_C)), per_b((HD, H_C)),
                      per_b((1, LANE)),
                      pl.BlockSpec((4, HD), lambda b, p, pt: (0, 0))] + pages,
            out_specs=[per_b((H_A, 2 * HD)), per_b((H_C, HD))],
            scratch_shapes=[pltpu.VMEM((rows_a, 2 * HD), F32), pltpu.VMEM((rows_a, 1), F32),
                            pltpu.VMEM((rows_a, 1), F32), pltpu.VMEM((rows_a, 2 * HD), F32),
                            pltpu.VMEM((H_C, HD, CHUNK), F32), pltpu.VMEM((H_C, 1), F32),
                            pltpu.VMEM((H_C, 1), F32), pltpu.VMEM((H_C, HD, CHUNK), F32),
                            pltpu.VMEM((16, 1), F32),
                            pltpu.VMEM((H_C, n_pg * CHUNK), F32), pltpu.VMEM((H_C, n_pg * CHUNK), F32)]),
        out_shape=[jax.ShapeDtypeStruct((nb, H_A, 2 * HD), F32), jax.ShapeDtypeStruct((nb, H_C, HD), F32)],
        compiler_params=_cparams("parallel", "arbitrary"),
        name="decode_paged_attention",
    )(page_table.reshape(-1), heads(qa, H_A), heads(ka, H_A), heads(va, H_A),
      qc3, kc3, jnp.swapaxes(qc3, 1, 2), jnp.swapaxes(vc3, 1, 2), lf.reshape(nb, 1, LANE), dl,
      *([cak] * n_pg), *([cav] * n_pg), *([cck] * n_pg), *([ccv] * n_pg), *([clf_t] * n_pg))


def _sk4_kernel(oa_ref, oc_ref, ub_ref, vb_ref, ws0_ref, bs0_ref, gsub_ref, gb_ref, gc_ref, w_ref,
                m_ref, cat, *, lam_init):
    j = pl.program_id(0)

    @pl.when(j == 0)
    def _():
        oa = oa_ref[...]
        parts = []
        for h in range(H_A):
            parts.append(_rms(oa[:, h * 2 * HD:(h + 1) * 2 * HD], gsub_ref[...]) * (1.0 - lam_init))
        parts.append(_rms(ub_ref[...] * (ws0_ref[...] * vb_ref[...] + bs0_ref[...]), gb_ref[...]))
        parts.append(_rms(oc_ref[...], gc_ref[...]))
        cat[...] = jnp.concatenate(parts, axis=1)

    m_ref[...] = _dot3(cat[...], w_ref[0])


def _sk4_call(layer, oa, oc, ub, vb, ws0, bs0, gsub, gb, gc, w_out, lam_init):
    nb = oa.shape[0]
    full = lambda shp: pl.BlockSpec(shp, lambda j: (0,) * len(shp))
    return pl.pallas_call(
        functools.partial(_sk4_kernel, lam_init=lam_init),
        grid=(D // TN_S,),
        in_specs=[full((nb, SEG)), full((nb, SEG)), full((nb, DB_W)), full((nb, DB_W)),
                  full((1, DB_W)), full((1, DB_W)), full((1, 2 * HD)), full((1, DB_W)), full((1, SEG)),
                  pl.BlockSpec((1, D, TN_S), lambda j: (layer, 0, j))],
        out_specs=pl.BlockSpec((nb, TN_S), lambda j: (0, j)),
        out_shape=jax.ShapeDtypeStruct((nb, D), F32),
        scratch_shapes=[pltpu.VMEM((nb, D), F32)],
        compiler_params=_cparams("arbitrary"),
        name="decode_out_proj",
    )(oa, oc, ub, vb, ws0, bs0, gsub, gb, gc, w_out)


def _sk4b_kernel(m_ref, x_ref, g1_ref, ga1_ref, g2_ref, sc2_ref, sh2_ref, wr_ref, br_ref,
                 x1_ref, h2_ref, gate_ref):
    x1 = x_ref[...] + ga1_ref[...] * _rms(m_ref[...], g1_ref[...])
    x1_ref[...] = x1
    h2 = _prenorm(x1, g2_ref[...], sc2_ref[...], sh2_ref[...])
    h2_ref[...] = h2
    e1, e2, w1, w2 = _route(_dot3(h2, wr_ref[...]) + br_ref[...])
    lane = lax.broadcasted_iota(I32, gate_ref.shape, 1).astype(F32)
    gate_ref[...] = jnp.where(lane == e1, w1, 0.0) + jnp.where(lane == e2, w2, 0.0)


def _sk4b_call(m, x, g1, ga1, g2, sc2, sh2, wr, br):
    nb = x.shape[0]
    return pl.pallas_call(
        _sk4b_kernel,
        out_shape=[jax.ShapeDtypeStruct((nb, D), F32), jax.ShapeDtypeStruct((nb, D), F32),
                   jax.ShapeDtypeStruct((nb, LANE), F32)],
        name="decode_post_router",
    )(m, x, g1, ga1, g2, sc2, sh2, wr, br)


def _sk5_kernel(el_ref, cnt_ref, h2_ref, gate_ref, x1_ref, ga_ref, g3_ref, wgu_ref, wd_ref,
                x2_ref, acc):
    s = pl.program_id(0)

    @pl.when(s == 0)
    def _():
        acc[...] = jnp.zeros_like(acc)

    @pl.when(s < cnt_ref[0])
    def _():
        hu = _dot3(h2_ref[...], wgu_ref[0, 0])
        lane = lax.broadcasted_iota(I32, gate_ref.shape, 1)
        g = jnp.sum(jnp.where(lane == el_ref[s], gate_ref[...], 0.0), axis=1, keepdims=True)
        hid = _silu(hu[:, :D_EXP]) * hu[:, D_EXP:] * g
        acc[...] += _dot3(hid, wd_ref[0, 0])

    @pl.when(s == pl.num_programs(0) - 1)
    def _():
        x2_ref[...] = x1_ref[...] + ga_ref[...] * _rms(acc[...], g3_ref[...])


def _sk5_call(layer, elist, cnt, h2, gate, x1, ga2, g3, w_gu, w_down):
    nb = h2.shape[0]
    full = lambda shp: pl.BlockSpec(shp, lambda s, el, cnt: (0,) * len(shp))
    return pl.pallas_call(
        _sk5_kernel,
        grid_spec=pltpu.PrefetchScalarGridSpec(
            num_scalar_prefetch=2, grid=(N_EXP,),
            in_specs=[full((nb, D)), full((nb, LANE)), full((nb, D)), full((nb, D)), full((1, D)),
                      pl.BlockSpec((1, 1, D, 2 * D_EXP), lambda s, el, cnt: (layer, el[s], 0, 0)),
                      pl.BlockSpec((1, 1, D_EXP, D), lambda s, el, cnt: (layer, el[s], 0, 0))],
            out_specs=full((nb, D)),
            scratch_shapes=[pltpu.VMEM((nb, D), F32)]),
        out_shape=jax.ShapeDtypeStruct((nb, D), F32),
        compiler_params=_cparams("arbitrary"),
        name="decode_moe",
    )(elist, cnt, h2, gate, x1, ga2, g3, w_gu, w_down)


def _rope_tables(pos):
    half = ROT_DIM // 2
    n = pos.shape[0]
    inv = ROPE_THETA ** (-jnp.arange(0, ROT_DIM, 2, dtype=F32) / ROT_DIM)
    ang = pos.astype(F32)[:, None] * inv[None, :]
    cos, sin = jnp.cos(ang), jnp.sin(ang)
    one = jnp.ones((n, HD - ROT_DIM), F32)
    zero = jnp.zeros((n, HD - ROT_DIM), F32)
    z8 = jnp.zeros((n, half), F32)
    c64 = jnp.concatenate([cos, cos, one], axis=1)
    sa64 = jnp.concatenate([-sin, z8, zero], axis=1)
    sb64 = jnp.concatenate([z8, sin, zero], axis=1)
    rep = lambda t: jnp.concatenate([t, t], axis=1)
    return rep(c64), rep(sa64), rep(sb64)


def _pad_lanes(a):
    return jnp.pad(a, ((0, 0), (0, LANE - a.shape[1])))


def kernel(x_prompt, x_sample, c_prompt, c_sample, cache_a_k, cache_a_v, cache_c_k, cache_c_v,
           cache_c_logf, page_table, w_ada, b_ada, g_norm, w_in, b_f, g_sgu, w_s, b_s,
           diff_lambda, g_subln, g_out_b, g_out_c, w_out, w_rg, b_rg, w_re, b_re, w_gu, w_down):
    depth = w_ada.shape[0]
    batch, seq, _ = x_prompt.shape
    nb, dec_len, _ = x_sample.shape
    assert dec_len == 1 and seq % CHUNK == 0
    n_pool, page = cache_a_k.shape[1], cache_a_k.shape[2]
    assert page == CHUNK
    past = page_table.shape[1] * page
    n = batch * seq

    rows_c = 16
    c_all = jnp.zeros((rows_c, D), F32).at[:batch].set(c_prompt).at[batch:batch + nb].set(c_sample)
    mod = _mod_call(c_all, w_ada, b_ada)

    tabs_p = _rope_tables(jnp.arange(seq))
    tabs_s = _rope_tables(jnp.full((1,), past))
    cak = jnp.transpose(cache_a_k, (0, 1, 3, 2, 4))
    cav = jnp.transpose(cache_a_v, (0, 1, 3, 2, 4))
    cck = jnp.transpose(cache_c_k, (0, 1, 3, 4, 2))
    ccv = jnp.transpose(cache_c_v, (0, 1, 3, 4, 2))
    clf_t = jnp.pad(jnp.swapaxes(cache_c_logf, 2, 3), ((0, 0), (0, 0), (0, 16 - H_C), (0, 0)))

    tq = min(TQ, seq)
    nq = seq // tq
    xp = x_prompt.reshape(n, D)
    xs = x_sample.reshape(nb, D)
    outs = [[] for _ in range(11)]
    stacked = None
    for l in range(depth):
        lam_init = 0.8 - 0.6 * math.exp(-0.3 * l)
        row = lambda v: v.reshape(1, -1)
        gn = [row(g_norm[l, k]) for k in range(4)]
        mp = [mod[l, :batch, k * D:(k + 1) * D].reshape(batch, 1, D) for k in range(6)]
        ms = [mod[l, batch:batch + nb, k * D:(k + 1) * D] for k in range(6)]
        wl = w_in[l]
        w768 = jnp.concatenate([wl[:, :OFF_B], wl[:, OFF_C:OFF_F]], axis=1).astype(BF16)
        wf = _pad_lanes(wl[:, OFF_F:])
        wb = jnp.concatenate([wl[:, OFF_B:OFF_C], wf], axis=1).astype(BF16)
        bf = _pad_lanes(row(b_f[l]))
        gs = row(g_sgu[l])
        wr = _pad_lanes(jnp.concatenate([w_rg[l], w_re[l]], axis=1))
        br = _pad_lanes(row(jnp.concatenate([b_rg[l], b_re[l]])))
        gsub, gb, gc = row(g_subln[l]), row(g_out_b[l]), row(g_out_c[l])
        dl = diff_lambda[l]

        qa, kaf, kab, vaf, vat, qc, kcf, kcb, vcf, vct = _ip1_call(
            xp, gn[0], mp[1], mp[0], w768, tabs_p, batch, seq, l, depth, stacked)
        stacked = (kaf, vaf, kcf, vcf)
        ub, vb, lfp, dcum = _ip2_call(xp, gn[0], mp[1], mp[0], wb, bf, gs, seq)
        oa = _flash_call("a", qa, kab, vat, (dl, gsub.reshape(2 * HD, 1)), batch, seq, lam_init)
        dh = (dcum[:, :H_C] * LOG2E).reshape(batch, nq, tq, H_C // 2, 2)
        dq = jnp.transpose(dh, (0, 3, 1, 4, 2))
        dk = jnp.transpose(dh, (0, 3, 1, 2, 4)).reshape(batch, H_C // 2, seq, 2)
        oc = _flash_call("c", qc, kcb, vct, (dq, dk), batch, seq)
        bst = jnp.repeat(jnp.swapaxes(b_s[l], 0, 1), HD, axis=1)
        x1, h2, eid, ew = _op_call(oa, oc, ub, vb, w_s[l], bst, gb, gc, w_out[l].astype(BF16),
                                   xp, gn[1], mp[2], gn[2], mp[4], mp[3], wr.astype(BF16), br, seq)
        tok_of_slot, tile_e, n_valid, dest = _moe_plan(eid[:, :2])
        y = _moe_call(l, tile_e, n_valid, tok_of_slot, h2, w_gu, w_down)
        xp = _combine_call(dest, y, ew, x1, mp[5], gn[3], seq)
        outs[4].append(lfp[:, :H_C])

        z, zf = _sk1_call(l, xs, gn[0], ms[1], ms[0], w_in, wf)
        qa_s, ka_s, va_s, ub_s, vb_s, qc_s, kc_s, vc_s, lf_s = _sk2_call(z, zf, tabs_s, bf, gs)
        oa_s, oc_s = _sk3_call(l, page_table, qa_s, ka_s, va_s, qc_s, kc_s, vc_s, lf_s, dl,
                               cak, cav, cck, ccv, clf_t, lam_init)
        ws0 = row(jnp.repeat(w_s[l, :, 0, 0], HD))
        bs0 = row(jnp.repeat(b_s[l, :, 0], HD))
        m_s = _sk4_call(l, oa_s.reshape(nb, SEG), oc_s.reshape(nb, SEG), ub_s, vb_s, ws0, bs0,
                        gsub, gb, gc, w_out, lam_init)
        x1_s, h2_s, gate = _sk4b_call(m_s, xs, gn[1], ms[2], gn[2], ms[4], ms[3], wr, br)
        used = jnp.any(gate[:, :N_EXP] > 0.0, axis=0)
        order = jnp.argsort(jnp.logical_not(used), stable=True).astype(I32)
        cnt = jnp.sum(used).astype(I32)
        elist = jnp.where(jnp.arange(N_EXP) < cnt, order, order[jnp.maximum(cnt - 1, 0)])
        xs = _sk5_call(l, elist, cnt.reshape(1), h2_s, gate, x1_s, ms[5], gn[3], w_gu, w_down)
        for k, a in enumerate((ka_s, va_s, kc_s, vc_s, lf_s[:, :H_C], vb_s)):
            outs[5 + k].append(a)

    st = lambda k: jnp.stack(outs[k])
    return (xp.reshape(batch, seq, D), xs.reshape(nb, 1, D),
            jnp.transpose(stacked[0], (0, 1, 3, 2, 4)), jnp.transpose(stacked[1], (0, 1, 3, 2, 4)),
            jnp.transpose(stacked[2], (0, 1, 4, 2, 3)), jnp.transpose(stacked[3], (0, 1, 4, 2, 3)),
            st(4).reshape(depth, batch, seq, H_C),
            st(5).reshape(depth, nb, 1, H_A, 2 * HD), st(6).reshape(depth, nb, 1, H_A, 2 * HD),
            st(7).reshape(depth, nb, 1, H_C, HD), st(8).reshape(depth, nb, 1, H_C, HD),
            st(9).reshape(depth, nb, 1, H_C), st(10).reshape(depth, nb, 1, G_B, HD))
```

```python
import functools
import math

import jax
import jax.numpy as jnp
from jax import lax
from jax.experimental import pallas as pl
from jax.experimental.pallas import tpu as pltpu

F32 = jnp.float32
BF16 = jnp.bfloat16
I32 = jnp.int32
U32 = jnp.uint32

D = 2048
HD = 64
H_A = 6
G_B = 8
DB_W = 512
CHUNK = 128
H_C = 12
SEG = 768
N_SEG = 6
OFF_B = 3 * SEG
OFF_C = OFF_B + 2 * DB_W
OFF_F = OFF_C + 3 * SEG
ROT_DIM = 16
ROPE_THETA = 500000.0
N_GROUPS = 4
N_EXP_PER_GROUP = 4
N_EXP = 16
D_EXP = 512
EPS = 1e-6
LANE = 128
NEG = -1e30
LOG2E = 1.4426950408889634
VMEM_LIMIT = 56 * 1024 * 1024

TM_IP = 512
TQ = 1024
TM_OP = 256
TM_E = 256
TM_CB = 256
TN_S = 512
PAGES_PER_STEP = 8
ROWS_A = 16


def _cparams(*sem):
    return pltpu.CompilerParams(dimension_semantics=sem, vmem_limit_bytes=VMEM_LIMIT)


def _dot(a, b):
    return jnp.dot(a, b, preferred_element_type=F32)


def _dot_nt(a, b):
    return lax.dot_general(a, b, (((1,), (1,)), ((), ())), preferred_element_type=F32)


def _split2(x):
    hi = x.astype(BF16)
    lo = (x - hi.astype(F32)).astype(BF16)
    return hi, lo


def _dot3(x, w):
    m = x.shape[0]
    xh, xl = _split2(x)
    wh, wl = _split2(w)
    y = _dot(jnp.concatenate([xh, xl], axis=0), wh)
    return y[:m] + y[m:] + _dot(xh, wl)


def _dot_exact_lhs(a_bf16, x):
    n = x.shape[1]
    h1 = x.astype(BF16)
    r1 = x - h1.astype(F32)
    h2 = r1.astype(BF16)
    h3 = (r1 - h2.astype(F32)).astype(BF16)
    y = _dot(a_bf16, jnp.concatenate([h1, h2, h3], axis=1))
    return y[:, :n] + y[:, n:2 * n] + y[:, 2 * n:]


def _rms(x, g):
    return x * lax.rsqrt(jnp.mean(x * x, axis=-1, keepdims=True) + EPS) * g


def _prenorm(x, g, sc, sh):
    return _rms(x, g) * (1.0 + sc) + sh


def _silu(x):
    return x * (1.0 / (1.0 + jnp.exp(-x)))


def _gelu(x):
    return x * (0.5 * (1.0 + jnp.tanh(0.7978845608028654 * (x + 0.044715 * (x * x * x)))))


def _log_sigmoid(x):
    return jnp.minimum(x, 0.0) - jnp.log1p(jnp.exp(-jnp.abs(x)))


def _rope(z, c, sa, sb):
    outs = []
    for k in range(z.shape[1] // LANE):
        zz = z[:, k * LANE:(k + 1) * LANE]
        outs.append(zz * c + pltpu.roll(zz, LANE - ROT_DIM // 2, 1) * sa
                    + pltpu.roll(zz, ROT_DIM // 2, 1) * sb)
    return jnp.concatenate(outs, axis=1)


def _route(logits):
    lane = lax.broadcasted_iota(I32, logits.shape, 1).astype(F32)
    gl = jnp.where(lane < N_GROUPS, logits, NEG)
    gmax = jnp.max(gl, axis=1, keepdims=True)
    grp = jnp.min(jnp.where(gl == gmax, lane, float(LANE)), axis=1, keepdims=True)
    p_grp = 1.0 / jnp.sum(jnp.exp(gl - gmax), axis=1, keepdims=True)
    lo = N_GROUPS + N_EXP_PER_GROUP * grp
    el = jnp.where((lane >= lo) & (lane < lo + N_EXP_PER_GROUP), logits, NEG)
    v1 = jnp.max(el, axis=1, keepdims=True)
    i1 = jnp.min(jnp.where(el == v1, lane, float(LANE)), axis=1, keepdims=True)
    el2 = jnp.where(lane == i1, NEG, el)
    v2 = jnp.max(el2, axis=1, keepdims=True)
    i2 = jnp.min(jnp.where(el2 == v2, lane, float(LANE)), axis=1, keepdims=True)
    t = jnp.exp(v2 - v1)
    w1 = p_grp * (1.0 / (1.0 + t))
    w2 = p_grp * (t / (1.0 + t))
    return i1 - N_GROUPS, i2 - N_GROUPS, w1, w2


def _mod_kernel(c_ref, w_ref, b_ref, o_ref):
    o_ref[0] = _dot3(_silu(c_ref[...]), w_ref[0]) + b_ref[0]


def _mod_call(c_all, w_ada, b_ada):
    depth, _, n6 = w_ada.shape
    rows = c_all.shape[0]
    tn = 1024
    return pl.pallas_call(
        _mod_kernel,
        grid=(depth, n6 // tn),
        in_specs=[pl.BlockSpec((rows, D), lambda l, j: (0, 0)),
                  pl.BlockSpec((1, D, tn), lambda l, j: (l, 0, j)),
                  pl.BlockSpec((1, 1, tn), lambda l, j: (l, 0, j))],
        out_specs=pl.BlockSpec((1, rows, tn), lambda l, j: (l, 0, j)),
        out_shape=jax.ShapeDtypeStruct((depth, rows, n6), F32),
        compiler_params=_cparams("parallel", "parallel"),
        name="adaln_mod",
    )(c_all, w_ada, b_ada.reshape(depth, 1, n6))


def _ip1_kernel(x_ref, g_ref, sc_ref, sh_ref, w_ref, cos_ref, sa_ref, sb_ref, *rest):
    (qa_ref, kaf_ref, kab_ref, vaf_ref, vat_ref,
     qc_ref, kcf_ref, kcb_ref, vcf_ref, vct_ref, h_scr) = rest[-11:]
    j = pl.program_id(1)
    q_scale = HD ** -0.5 * LOG2E

    @pl.when(j == 0)
    def _():
        h_scr[...] = _prenorm(x_ref[...], g_ref[...], sc_ref[0], sh_ref[0]).astype(BF16)

    def segment():
        return _dot(h_scr[...], w_ref[...])

    @pl.when(j == 0)
    def _():
        qa_ref[...] = (_rope(segment(), cos_ref[...], sa_ref[...], sb_ref[...]) * q_scale).astype(BF16)

    @pl.when(j == 1)
    def _():
        k = _rope(segment(), cos_ref[...], sa_ref[...], sb_ref[...])
        for h in range(H_A):
            kaf_ref[0, 0, h] = k[:, h * 2 * HD:(h + 1) * 2 * HD]
        kab_ref[...] = k.astype(BF16)

    @pl.when(j == 2)
    def _():
        z = segment()
        for h in range(H_A):
            vaf_ref[0, 0, h] = z[:, h * 2 * HD:(h + 1) * 2 * HD]
        vat_ref[0] = z.T.astype(BF16)

    @pl.when(j == 3)
    def _():
        qc_ref[...] = (segment() * q_scale).astype(BF16)

    @pl.when(j == 4)
    def _():
        z = segment()
        zt = z.T
        for h in range(H_C):
            kcf_ref[0, 0, h] = zt[h * HD:(h + 1) * HD, :]
        kcb_ref[...] = z.astype(BF16)

    @pl.when(j == 5)
    def _():
        zt = segment().T
        for h in range(H_C):
            vcf_ref[0, 0, h] = zt[h * HD:(h + 1) * HD, :]
        vct_ref[0] = zt.astype(BF16)


def _ip1_call(x, g0, sc, sh, w768, rope_tabs, batch, seq, layer, depth, stacked):
    n = x.shape[0]
    tm = min(TM_IP, seq)
    tpb = seq // tm
    bmap = lambda i, j: (i // tpb, 0, 0)
    row = pl.BlockSpec((tm, SEG), lambda i, j: (i, 0))
    tab = pl.BlockSpec((tm, LANE), lambda i, j: (i % tpb, 0))
    a_nat = pl.BlockSpec((1, 1, H_A, tm, 2 * HD), lambda i, j: (layer, i // tpb, 0, i % tpb, 0))
    c_nat = pl.BlockSpec((1, 1, H_C, HD, tm), lambda i, j: (layer, i // tpb, 0, 0, i % tpb))
    v_t = pl.BlockSpec((1, SEG, tm), lambda i, j: (i, 0, 0))
    b = jax.ShapeDtypeStruct((n, SEG), BF16)
    fa = jax.ShapeDtypeStruct((depth, batch, H_A, seq, 2 * HD), F32)
    fc = jax.ShapeDtypeStruct((depth, batch, H_C, HD, seq), F32)
    stacked = () if stacked is None else tuple(stacked)
    n_in = 8
    bt = jax.ShapeDtypeStruct((n // tm, SEG, tm), BF16)
    return pl.pallas_call(
        _ip1_kernel,
        grid=(n // tm, N_SEG),
        in_specs=[pl.BlockSpec((tm, D), lambda i, j: (i, 0)),
                  pl.BlockSpec((1, D), lambda i, j: (0, 0)),
                  pl.BlockSpec((1, 1, D), bmap), pl.BlockSpec((1, 1, D), bmap),
                  pl.BlockSpec((D, SEG), lambda i, j: (0, j)),
                  tab, tab, tab] + [pl.BlockSpec(memory_space=pl.ANY)] * len(stacked),
        out_specs=[row, a_nat, row, a_nat, v_t, row, c_nat, row, c_nat, v_t],
        out_shape=[b, fa, b, fa, bt, b, fc, b, fc, bt],
        input_output_aliases={n_in + k: o for k, o in enumerate((1, 3, 6, 8)[:len(stacked)])},
        scratch_shapes=[pltpu.VMEM((tm, D), BF16)],
        compiler_params=_cparams("parallel", "arbitrary"),
        name="prompt_in_proj",
    )(x, g0, sc, sh, w768, *rope_tabs, *stacked)


def _ip2_kernel(x_ref, g_ref, sc_ref, sh_ref, w_ref, bf_ref, gs_ref,
                ub_ref, vb_ref, lf_ref, d_ref, carry, *, tpb):
    i = pl.program_id(0)
    tm = x_ref.shape[0]
    h = _prenorm(x_ref[...], g_ref[...], sc_ref[0], sh_ref[0]).astype(BF16)
    z = _dot(h, w_ref[...])
    ub_ref[...] = _gelu(z[:, :DB_W])
    vb_ref[...] = _rms(_gelu(z[:, DB_W:2 * DB_W]), gs_ref[...]).astype(BF16)
    lf = _log_sigmoid(z[:, 2 * DB_W:] + bf_ref[...])
    lf_ref[...] = lf

    @pl.when(i % tpb == 0)
    def _():
        carry[...] = jnp.zeros_like(carry)

    r = lax.broadcasted_iota(I32, (tm, tm), 0)
    c = lax.broadcasted_iota(I32, (tm, tm), 1)
    d = _dot_exact_lhs((r >= c).astype(BF16), lf) + carry[...]
    d_ref[...] = d
    carry[...] = d[tm - 1:tm, :]


def _ip2_call(x, g0, sc, sh, wb, bf, gs, seq):
    n = x.shape[0]
    tm = min(TM_IP, seq)
    tpb = seq // tm
    bmap = lambda i: (i // tpb, 0, 0)
    full = lambda shp: pl.BlockSpec(shp, lambda i: (0,) * len(shp))
    return pl.pallas_call(
        functools.partial(_ip2_kernel, tpb=tpb),
        grid=(n // tm,),
        in_specs=[pl.BlockSpec((tm, D), lambda i: (i, 0)), full((1, D)),
                  pl.BlockSpec((1, 1, D), bmap), pl.BlockSpec((1, 1, D), bmap),
                  full((D, 2 * DB_W + LANE)), full((1, LANE)), full((1, DB_W))],
        out_specs=[pl.BlockSpec((tm, DB_W), lambda i: (i, 0)),
                   pl.BlockSpec((tm, DB_W), lambda i: (i, 0)),
                   pl.BlockSpec((tm, LANE), lambda i: (i, 0)),
                   pl.BlockSpec((tm, LANE), lambda i: (i, 0))],
        out_shape=[jax.ShapeDtypeStruct((n, DB_W), F32), jax.ShapeDtypeStruct((n, DB_W), BF16),
                   jax.ShapeDtypeStruct((n, LANE), F32), jax.ShapeDtypeStruct((n, LANE), F32)],
        scratch_shapes=[pltpu.VMEM((1, LANE), F32)],
        compiler_params=_cparams("arbitrary"),
        name="prompt_sgu_gate_proj",
    )(x, g0, sc, sh, wb, bf, gs)


def _flash_kernel(*refs, mode, tq, tv, lam_init):
    if mode == "a":
        q_ref, k_ref, vt_ref, dl_ref, gs_ref, o_ref, qq, m_s, acc = refs
    else:
        q_ref, k_ref, vt_ref, dq_ref, dk_ref, o_ref, qq, m_s, acc = refs
    qi = pl.program_id(2)
    q = q_ref[...]
    lane = lax.broadcasted_iota(I32, q.shape, 1)
    qq[0:tq, :] = jnp.where(lane < HD, q, jnp.zeros_like(q))
    qq[tq:2 * tq, :] = jnp.where(lane >= HD, q, jnp.zeros_like(q))
    m_s[...] = jnp.full_like(m_s, NEG)
    acc[...] = jnp.zeros_like(acc)
    ones = jnp.ones((16, tv), BF16)
    if mode == "c":
        dq = dq_ref[0, 0, 0]
        dq_row = jnp.concatenate([dq[0:1, :], dq[1:2, :]], axis=1)

    full = ((0, 2 * tq),)
    half = tq // 2
    split_diag = half % tv == 0

    def take(ref, cols):
        if cols == full:
            return ref[...]
        return jnp.concatenate([ref[:, a:a + w] for a, w in cols], axis=1)

    def scores(kb, k0, kn, cols):
        start = pl.multiple_of(kb * tq + k0, tv)
        k = k_ref[pl.ds(start, kn), :]
        if cols == full:
            q_rows = qq[...]
        else:
            q_rows = jnp.concatenate([qq[a:a + w, :] for a, w in cols], axis=0)
        s = _dot_nt(k, q_rows)
        if mode == "c":
            dk = dk_ref[0, 0, pl.ds(start, kn), :]
            ranges = ((0, tq), (tq, tq)) if cols == full else cols
            dk_cols = jnp.concatenate(
                [jnp.broadcast_to(dk[:, a // tq:a // tq + 1], (kn, w)) for a, w in ranges], axis=1)
            s = s + (take_row(dq_row, cols) - dk_cols)
        return s

    def take_row(row, cols):
        if cols == full:
            return row
        return jnp.concatenate([row[:, a:a + w] for a, w in cols], axis=1)

    def causal(s, kn, width):
        r = lax.broadcasted_iota(I32, (kn, width), 0)
        c = lax.broadcasted_iota(I32, (kn, width), 1)
        return jnp.where(jnp.concatenate([r <= c, r <= c], axis=1), s, NEG)

    def update(s, kb, k0, cols):
        m_old = take(m_s, cols)
        m_new = jnp.maximum(m_old, jnp.max(s, axis=0, keepdims=True))
        alpha = jnp.exp2(m_old - m_new)
        pb = jnp.exp2(s - m_new).astype(BF16)
        pv = None
        for j in range(s.shape[0] // tv):
            blk = kb * (tq // tv) + (k0 // tv + j)
            t = _dot(jnp.concatenate([vt_ref[blk], ones], axis=0), pb[j * tv:(j + 1) * tv])
            pv = t if pv is None else pv + t
        new_acc = alpha * take(acc, cols) + pv
        off = 0
        for a, w in cols:
            acc[:, a:a + w] = new_acc[:, off:off + w]
            m_s[:, a:a + w] = m_new[:, off:off + w]
            off += w

    def body(kb, carry):
        update(scores(kb, 0, tq, full), kb, 0, full)
        return carry

    lax.fori_loop(0, qi, body, 0)
    if split_diag:
        late = ((half, half), (tq + half, half))
        update(causal(scores(qi, 0, half, full), half, tq), qi, 0, full)
        update(causal(scores(qi, half, half, late), half, half), qi, half, late)
    else:
        update(causal(scores(qi, 0, tq, full), tq, tq), qi, 0, full)

    o = acc[0:LANE, :] * (1.0 / acc[LANE:LANE + 1, :])
    if mode == "a":
        dl = dl_ref[...]
        lam = (jnp.exp(jnp.sum(dl[0:1] * dl[1:2], axis=1, keepdims=True))
               - jnp.exp(jnp.sum(dl[2:3] * dl[3:4], axis=1, keepdims=True)) + lam_init)
        oa = o[:, 0:tq] - lam * o[:, tq:2 * tq]
        oa = oa * lax.rsqrt(jnp.mean(oa * oa, axis=0, keepdims=True) + EPS) * gs_ref[...]
        o_ref[...] = (oa * (1.0 - lam_init)).T.astype(o_ref.dtype)
    else:
        d_row = lax.broadcasted_iota(I32, (2 * HD, tq), 0)
        o_ref[...] = jnp.where(d_row < HD, o[:, 0:tq], o[:, tq:2 * tq]).T


def _flash_call(mode, q, k, v, extra, batch, seq, lam_init=0.0):
    n = q.shape[0]
    tq = min(TQ, seq)
    tv = min(TM_IP, seq)
    nq = seq // tq
    qspec = pl.BlockSpec((tq, LANE), lambda b, g, i: (b * nq + i, g))
    kspec = pl.BlockSpec((seq, LANE), lambda b, g, i: (b, g))
    vtspec = pl.BlockSpec((seq // tv, LANE, tv), lambda b, g, i: (b, g, 0))
    if mode == "a":
        especs = [pl.BlockSpec((4, HD), lambda b, g, i: (0, 0)),
                  pl.BlockSpec((LANE, 1), lambda b, g, i: (0, 0))]
        out_dtype = BF16
    else:
        especs = [pl.BlockSpec((1, 1, 1, 2, tq), lambda b, g, i: (b, g, i, 0, 0)),
                  pl.BlockSpec((1, 1, seq, 2), lambda b, g, i: (b, g, 0, 0))]
        out_dtype = F32
    return pl.pallas_call(
        functools.partial(_flash_kernel, mode=mode, tq=tq, tv=tv, lam_init=lam_init),
        grid=(batch, SEG // LANE, nq),
        in_specs=[qspec, kspec, vtspec] + especs,
        out_specs=qspec,
        out_shape=jax.ShapeDtypeStruct((n, SEG), out_dtype),
        scratch_shapes=[pltpu.VMEM((2 * tq, LANE), BF16), pltpu.VMEM((1, 2 * tq), F32),
                        pltpu.VMEM((LANE + 16, 2 * tq), F32)],
        compiler_params=_cparams("parallel", "parallel", "parallel"),
        name="prompt_flash_" + mode,
    )(q, k, v, *extra)


def _op_kernel(oa_ref, oc_ref, ub_ref, vb_ref, ws_ref, bst_ref, gb_ref, gc_ref, wout_ref,
               x_ref, g1_ref, ga1_ref, g2_ref, sc2_ref, sh2_ref, wr_ref, br_ref,
               x1_ref, h2_ref, eid_ref, ew_ref, cat):
    tm = x_ref.shape[0]
    r = lax.broadcasted_iota(I32, (CHUNK, CHUNK), 0)
    c = lax.broadcasted_iota(I32, (CHUNK, CHUNK), 1)
    lane_b = lax.broadcasted_iota(I32, (CHUNK, DB_W), 1)
    w_tril = [jnp.where(r >= c, ws_ref[g], 0.0).astype(BF16) for g in range(G_B)]
    for ch in range(tm // CHUNK):
        rows = slice(ch * CHUNK, (ch + 1) * CHUNK)
        vb = vb_ref[rows, :]
        mix = bst_ref[...]
        for g in range(G_B):
            in_group = (lane_b >= g * HD) & (lane_b < (g + 1) * HD)
            mix = mix + jnp.where(in_group, _dot(w_tril[g], vb), 0.0)
        cat[rows, SEG:SEG + DB_W] = _rms(ub_ref[rows, :] * mix, gb_ref[...]).astype(BF16)
    cat[:, 0:SEG] = oa_ref[...]
    cat[:, SEG + DB_W:] = _rms(oc_ref[...], gc_ref[...]).astype(BF16)
    m = _dot(cat[...], wout_ref[...])
    x1 = x_ref[...] + ga1_ref[0] * _rms(m, g1_ref[...])
    x1_ref[...] = x1
    h2 = _prenorm(x1, g2_ref[...], sc2_ref[0], sh2_ref[0]).astype(BF16)
    lo = lax.bitcast_convert_type(h2[:, :D // 2].astype(F32), U32)
    hi = lax.bitcast_convert_type(h2[:, D // 2:].astype(F32), U32)
    h2_ref[...] = (hi & jnp.uint32(0xFFFF0000)) | (lo >> 16)
    e1, e2, w1, w2 = _route(_dot(h2, wr_ref[...]) + br_ref[...])
    lane = lax.broadcasted_iota(I32, (tm, LANE), 1)
    eid_ref[...] = jnp.where(lane == 0, e1, jnp.where(lane == 1, e2, 0.0)).astype(I32)
    ew_ref[...] = jnp.where(lane == 0, w1, jnp.where(lane == 1, w2, 0.0))


def _op_call(oa, oc, ub, vb, ws, bst, gb, gc, wout, x, g1, ga1, g2, sc2, sh2, wr, br, seq):
    n = x.shape[0]
    tm = min(TM_OP, seq)
    tpb = seq // tm
    bmap = lambda i: (i // tpb, 0, 0)
    full = lambda shp: pl.BlockSpec(shp, lambda i: (0,) * len(shp))
    rows = lambda w: pl.BlockSpec((tm, w), lambda i: (i, 0))
    return pl.pallas_call(
        _op_kernel,
        grid=(n // tm,),
        in_specs=[rows(SEG), rows(SEG), rows(DB_W), rows(DB_W),
                  full((G_B, CHUNK, CHUNK)), full((CHUNK, DB_W)), full((1, DB_W)), full((1, SEG)),
                  full((D, D)), rows(D), full((1, D)), pl.BlockSpec((1, 1, D), bmap),
                  full((1, D)), pl.BlockSpec((1, 1, D), bmap), pl.BlockSpec((1, 1, D), bmap),
                  full((D, LANE)), full((1, LANE))],
        out_specs=[rows(D), rows(D // 2), rows(LANE), rows(LANE)],
        out_shape=[jax.ShapeDtypeStruct((n, D), F32), jax.ShapeDtypeStruct((n, D // 2), U32),
                   jax.ShapeDtypeStruct((n, LANE), I32), jax.ShapeDtypeStruct((n, LANE), F32)],
        scratch_shapes=[pltpu.VMEM((tm, D), BF16)],
        compiler_params=_cparams("parallel"),
        name="prompt_mix_out_router",
    )(oa, oc, ub, vb, ws, bst, gb, gc, wout, x, g1, ga1, g2, sc2, sh2, wr, br)


def _moe_kernel(te_ref, nv_ref, tok_ref, h2_ref, wgu_ref, wd_ref, y_ref,
                xbuf, wgu_b, wd_b, sem):
    t = pl.program_id(0)
    n_valid = nv_ref[0]
    valid = t < n_valid
    slot = lax.rem(t, 2)

    def for_rows(tile, buf_slot, act):
        def body(r, carry):
            act(pltpu.make_async_copy(h2_ref.at[pl.ds(tok_ref[tile * TM_E + r], 1)],
                                      xbuf.at[buf_slot, pl.ds(r, 1)], sem.at[buf_slot]))
            return carry
        lax.fori_loop(0, TM_E, body, 0, unroll=8)

    @pl.when(t == 0)
    def _():
        for_rows(0, 0, lambda cp: cp.start(priority=1))

    @pl.when(t + 1 < n_valid)
    def _():
        for_rows(t + 1, 1 - slot, lambda cp: cp.start(priority=1))

    @pl.when(valid & ((t == 0) | (te_ref[t] != te_ref[jnp.maximum(t - 1, 0)])))
    def _():
        wgu_b[...] = wgu_ref[0, 0].astype(BF16)
        wd_b[...] = wd_ref[0, 0].astype(BF16)

    @pl.when(valid)
    def _():
        for_rows(t, slot, lambda cp: cp.wait())
        u = xbuf[slot]
        x = jnp.concatenate([lax.bitcast_convert_type(u << 16, F32).astype(BF16),
                             lax.bitcast_convert_type(u & jnp.uint32(0xFFFF0000), F32).astype(BF16)], axis=1)
        hu = _dot(x, wgu_b[...])
        hid = _silu(hu[:, :D_EXP]) * hu[:, D_EXP:]
        y_ref[...] = _dot(hid.astype(BF16), wd_b[...])

    @pl.when(jnp.logical_not(valid))
    def _():
        y_ref[...] = jnp.zeros_like(y_ref)


def _moe_call(layer, tile_e, n_valid, tok_of_slot, h2, w_gu, w_down):
    total = tok_of_slot.shape[0]
    n_tiles = total // TM_E
    expert = lambda t, te, nv, tok: (layer, te[t], 0, 0)
    return pl.pallas_call(
        _moe_kernel,
        grid_spec=pltpu.PrefetchScalarGridSpec(
            num_scalar_prefetch=3, grid=(n_tiles,),
            in_specs=[pl.BlockSpec(memory_space=pl.ANY),
                      pl.BlockSpec((1, 1, D, 2 * D_EXP), expert),
                      pl.BlockSpec((1, 1, D_EXP, D), expert)],
            out_specs=pl.BlockSpec((TM_E, D), lambda t, te, nv, tok: (t, 0)),
            scratch_shapes=[pltpu.VMEM((2, TM_E, D // 2), U32),
                            pltpu.VMEM((D, 2 * D_EXP), BF16), pltpu.VMEM((D_EXP, D), BF16),
                            pltpu.SemaphoreType.DMA((2,))]),
        out_shape=jax.ShapeDtypeStruct((total, D), F32),
        compiler_params=_cparams("arbitrary"),
        name="moe_experts",
    )(tile_e, n_valid, tok_of_slot, h2, w_gu, w_down)


def _combine_kernel(pos_ref, y_ref, ew_ref, x1_ref, ga_ref, g3_ref, x2_ref, buf, sem, *, n_tok):
    i = pl.program_id(0)
    tm = x1_ref.shape[0]
    slot = lax.rem(i, 2)

    def for_rows(tile, buf_slot, act):
        def body(row, carry):
            for k in range(2):
                src = pos_ref[k * n_tok + tile * tm + row]
                act(pltpu.make_async_copy(y_ref.at[pl.ds(src, 1)], buf.at[buf_slot, k, pl.ds(row, 1)],
                                          sem.at[buf_slot]), k)
            return carry
        lax.fori_loop(0, tm, body, 0, unroll=8)

    @pl.when(i == 0)
    def _():
        for_rows(0, 0, lambda cp, k: cp.start(priority=k))

    @pl.when(i + 1 < pl.num_programs(0))
    def _():
        for_rows(i + 1, 1 - slot, lambda cp, k: cp.start(priority=k))

    for_rows(i, slot, lambda cp, k: cp.wait())
    ew = ew_ref[...]
    mo = ew[:, 0:1] * buf[slot, 0] + ew[:, 1:2] * buf[slot, 1]
    x2_ref[...] = x1_ref[...] + ga_ref[0] * _rms(mo, g3_ref[...])


def _combine_call(pos, y, ew, x1, ga2, g3, seq):
    n = x1.shape[0]
    tm = min(TM_CB, seq)
    tpb = seq // tm
    return pl.pallas_call(
        functools.partial(_combine_kernel, n_tok=n),
        grid_spec=pltpu.PrefetchScalarGridSpec(
            num_scalar_prefetch=1, grid=(n // tm,),
            in_specs=[pl.BlockSpec(memory_space=pl.ANY),
                      pl.BlockSpec((tm, LANE), lambda i, pos: (i, 0)),
                      pl.BlockSpec((tm, D), lambda i, pos: (i, 0)),
                      pl.BlockSpec((1, 1, D), lambda i, pos: (i // tpb, 0, 0)),
                      pl.BlockSpec((1, D), lambda i, pos: (0, 0))],
            out_specs=pl.BlockSpec((tm, D), lambda i, pos: (i, 0)),
            scratch_shapes=[pltpu.VMEM((2, 2, tm, D), F32), pltpu.SemaphoreType.DMA((2,))]),
        out_shape=jax.ShapeDtypeStruct((n, D), F32),
        compiler_params=_cparams("arbitrary"),
        name="moe_combine",
    )(pos, y, ew, x1, ga2, g3)


def _moe_plan(eid):
    n = eid.shape[0]
    pe = jnp.concatenate([eid[:, 0], eid[:, 1]])
    tok = jnp.concatenate([jnp.arange(n, dtype=I32)] * 2)
    onehot = (pe[:, None] == jnp.arange(N_EXP, dtype=I32)[None, :]).astype(I32)
    counts = jnp.sum(onehot, axis=0)
    rank = jnp.sum((jnp.cumsum(onehot, axis=0) - onehot) * onehot, axis=1)
    padded = ((counts + TM_E - 1) // TM_E) * TM_E
    ends = jnp.cumsum(padded)
    dest = (ends - padded)[pe] + rank
    total = 2 * n + N_EXP * TM_E
    tok_of_slot = jnp.zeros((total,), I32).at[dest].set(tok)
    n_valid = (ends[-1] // TM_E).astype(I32)
    starts = jnp.arange(total // TM_E, dtype=I32) * TM_E
    tile_e = jnp.minimum(jnp.sum((starts[:, None] >= ends[None, :]).astype(I32), axis=1), N_EXP - 1)
    tile_e = jnp.where(starts // TM_E < n_valid, tile_e, tile_e[jnp.maximum(n_valid - 1, 0)])
    return tok_of_slot, tile_e.astype(I32), n_valid.reshape(1), dest.astype(I32)


def _sk1_kernel(x_ref, g_ref, sc_ref, sh_ref, w_ref, wf_ref, z_ref, zf_ref, h_scr):
    j = pl.program_id(0)

    @pl.when(j == 0)
    def _():
        h = _prenorm(x_ref[...], g_ref[...], sc_ref[...], sh_ref[...])
        h_scr[...] = h
        zf_ref[...] = _dot3(h, wf_ref[...])

    z_ref[...] = _dot3(h_scr[...], w_ref[0])


def _sk1_call(layer, x, g0, sc, sh, w_in, wf):
    nb = x.shape[0]
    full = lambda shp: pl.BlockSpec(shp, lambda j: (0,) * len(shp))
    return pl.pallas_call(
        _sk1_kernel,
        grid=(OFF_F // TN_S,),
        in_specs=[full((nb, D)), full((1, D)), full((nb, D)), full((nb, D)),
                  pl.BlockSpec((1, D, TN_S), lambda j: (layer, 0, j)), full((D, LANE))],
        out_specs=[pl.BlockSpec((nb, TN_S), lambda j: (0, j)), full((nb, LANE))],
        out_shape=[jax.ShapeDtypeStruct((nb, OFF_F), F32), jax.ShapeDtypeStruct((nb, LANE), F32)],
        scratch_shapes=[pltpu.VMEM((nb, D), F32)],
        compiler_params=_cparams("arbitrary"),
        name="decode_in_proj",
    )(x, g0, sc, sh, w_in, wf)


def _sk2_kernel(z_ref, zf_ref, cos_ref, sa_ref, sb_ref, bf_ref, gs_ref,
                qa_ref, ka_ref, va_ref, ub_ref, vb_ref, qc_ref, kc_ref, vc_ref, lf_ref):
    c, sa, sb = cos_ref[...], sa_ref[...], sb_ref[...]
    qa_ref[...] = _rope(z_ref[:, 0:SEG], c, sa, sb) * HD ** -0.5
    ka_ref[...] = _rope(z_ref[:, SEG:2 * SEG], c, sa, sb)
    va_ref[...] = z_ref[:, 2 * SEG:OFF_B]
    ub_ref[...] = _gelu(z_ref[:, OFF_B:OFF_B + DB_W])
    vb_ref[...] = _rms(_gelu(z_ref[:, OFF_B + DB_W:OFF_C]), gs_ref[...])
    qc_ref[...] = z_ref[:, OFF_C:OFF_C + SEG] * HD ** -0.5
    kc_ref[...] = z_ref[:, OFF_C + SEG:OFF_C + 2 * SEG]
    vc_ref[...] = z_ref[:, OFF_C + 2 * SEG:OFF_F]
    lf_ref[...] = _log_sigmoid(zf_ref[...] + bf_ref[...])


def _sk2_call(z, zf, rope_tabs, bf, gs):
    nb = z.shape[0]
    s = lambda w: jax.ShapeDtypeStruct((nb, w), F32)
    return pl.pallas_call(
        _sk2_kernel,
        out_shape=[s(SEG), s(SEG), s(SEG), s(DB_W), s(DB_W), s(SEG), s(SEG), s(SEG), s(LANE)],
        name="decode_proj_post",
    )(z, zf, *rope_tabs, bf, gs)


def _lanes_sum_exact(acc):
    ones = jnp.ones((8, acc.shape[1]), BF16)
    h1 = acc.astype(BF16)
    r1 = acc - h1.astype(F32)
    h2 = r1.astype(BF16)
    h3 = (r1 - h2.astype(F32)).astype(BF16)
    y = _dot_nt(ones, h1) + _dot_nt(ones, h2) + _dot_nt(ones, h3)
    return y[0:1]


def _sk3_kernel(pt_ref, qa_ref, kan_ref, van_ref, qc_ref, kcn_ref, qct_ref, vct_ref, lfn_ref, dl_ref,
                *rest, lam_init, n_pg):
    ak, av, ck, cv, clf = (rest[i * n_pg:(i + 1) * n_pg] for i in range(5))
    (oa_ref, oc_ref, qa_s, m_a, l_a, acc_a, qc_s, m_c, l_c, acc_c, carry, s_scr, p_scr) = rest[5 * n_pg:]
    p = pl.program_id(1)
    rows_a = ROWS_A
    wide = H_A * CHUNK
    row = lax.broadcasted_iota(I32, (rows_a, 2 * HD), 0)
    lane = lax.broadcasted_iota(I32, (rows_a, 2 * HD), 1)
    pair = lax.shift_right_logical(row, 1)

    def rows_of(ref):
        out = jnp.zeros((rows_a, 2 * HD), F32)
        for h in range(H_A):
            out = jnp.where(pair == h, jnp.broadcast_to(ref[0, h:h + 1, :], (rows_a, 2 * HD)), out)
        return out

    @pl.when(p == 0)
    def _():
        second_map = lax.bitwise_and(row, 1) == 1
        q16 = jnp.where((lane < HD) != second_map, rows_of(qa_ref), 0.0)
        qa_s[...] = q16
        m_a[...] = jnp.sum(q16 * rows_of(kan_ref), axis=1, keepdims=True)
        l_a[...] = jnp.ones_like(l_a)
        acc_a[...] = rows_of(van_ref)
        m_c[...] = jnp.sum(qc_ref[0] * kcn_ref[0], axis=1, keepdims=True)
        l_c[...] = jnp.ones_like(l_c)
        lane1 = lax.broadcasted_iota(I32, (1, CHUNK), 1)
        for h in range(H_C):
            qc_s[h] = jnp.broadcast_to(qct_ref[0, :, h:h + 1], (HD, CHUNK))
            acc_c[h] = jnp.where(lane1 == 0, jnp.broadcast_to(vct_ref[0, :, h:h + 1], (HD, CHUNK)), 0.0)
        r16 = lax.broadcasted_iota(I32, (16, LANE), 0)
        l16 = lax.broadcasted_iota(I32, (16, LANE), 1)
        carry[...] = jnp.sum(jnp.where(r16 == l16, jnp.broadcast_to(lfn_ref[0], (16, LANE)), 0.0),
                             axis=1, keepdims=True)

    row_w = lax.broadcasted_iota(I32, (rows_a, wide), 0)
    col_w = lax.broadcasted_iota(I32, (rows_a, wide), 1)
    same_head = lax.shift_right_logical(col_w, 7) == lax.shift_right_logical(row_w, 1)
    qb = qa_s[...].astype(BF16)
    s_a = jnp.concatenate(
        [jnp.where(same_head, _dot_nt(qb, ak[k][0, 0].reshape(wide, 2 * HD).astype(BF16)), NEG)
         for k in range(n_pg)], axis=1)
    m_old = m_a[...]
    m_new = jnp.maximum(m_old, jnp.max(s_a, axis=1, keepdims=True))
    alpha = jnp.exp(m_old - m_new)
    pr = jnp.exp(s_a - m_new)
    l_a[...] = alpha * l_a[...] + jnp.sum(pr, axis=1, keepdims=True)
    pv = _dot(pr[:, 0:wide].astype(BF16), av[0][0, 0].reshape(wide, 2 * HD).astype(BF16))
    for k in range(1, n_pg):
        pv = pv + _dot(pr[:, k * wide:(k + 1) * wide].astype(BF16),
                       av[k][0, 0].reshape(wide, 2 * HD).astype(BF16))
    acc_a[...] = alpha * acc_a[...] + pv
    m_a[...] = m_new

    i_ = lax.broadcasted_iota(I32, (CHUNK, CHUNK), 0)
    j_ = lax.broadcasted_iota(I32, (CHUNK, CHUNK), 1)
    later_mask = (i_ > j_).astype(BF16)
    run = carry[...]
    for k in range(n_pg):
        lf = clf[k][0, 0]
        h1 = lf.astype(BF16)
        r1 = lf - h1.astype(F32)
        h2 = r1.astype(BF16)
        h3 = (r1 - h2.astype(F32)).astype(BF16)
        y = _dot(jnp.concatenate([h1, h2, h3], axis=0), later_mask)
        bias = run + (y[0:16] + y[16:32] + y[32:48])
        for h in range(H_C):
            s_scr[h:h + 1, k * CHUNK:(k + 1) * CHUNK] = (
                jnp.sum(ck[k][0, 0, h] * qc_s[h], axis=0, keepdims=True) + bias[h:h + 1, :])
        run = run + jnp.sum(lf, axis=1, keepdims=True)
    carry[...] = run
    s_c = s_scr[...]
    m_old = m_c[...]
    m_new = jnp.maximum(m_old, jnp.max(s_c, axis=1, keepdims=True))
    alpha = jnp.exp(m_old - m_new)
    pr = jnp.exp(s_c - m_new)
    l_c[...] = alpha * l_c[...] + jnp.sum(pr, axis=1, keepdims=True)
    m_c[...] = m_new
    p_scr[...] = pr
    for h in range(H_C):
        acc = alpha[h:h + 1, :] * acc_c[h]
        for k in range(n_pg):
            acc = acc + cv[k][0, 0, h] * p_scr[h:h + 1, k * CHUNK:(k + 1) * CHUNK]
        acc_c[h] = acc

    @pl.when(p == pl.num_programs(1) - 1)
    def _():
        dl = dl_ref[...]
        lam = (jnp.exp(jnp.sum(dl[0:1] * dl[1:2], axis=1, keepdims=True))
               - jnp.exp(jnp.sum(dl[2:3] * dl[3:4], axis=1, keepdims=True)) + lam_init)
        o = acc_a[...] * (1.0 / l_a[...])
        for h in range(H_A):
            oa_ref[0, h:h + 1, :] = o[2 * h:2 * h + 1] - lam * o[2 * h + 1:2 * h + 2]
        inv_l = 1.0 / l_c[...]
        for h in range(H_C):
            oc_ref[0, h:h + 1, :] = _lanes_sum_exact(acc_c[h]) * inv_l[h:h + 1, :]


def _sk3_call(layer, page_table, qa, ka, va, qc, kc, vc, lf, dl, cak, cav, cck, ccv, clf_t, lam_init):
    nb, n_pages = page_table.shape
    n_pg = math.gcd(n_pages, PAGES_PER_STEP)
    heads = lambda a, h: a.reshape(nb, h, a.shape[1] // h)
    qc3, kc3, vc3 = heads(qc, H_C), heads(kc, H_C), heads(vc, H_C)
    per_b = lambda shp: pl.BlockSpec((1,) + shp, lambda b, p, pt: (b, 0, 0))

    def page_spec(shp, k):
        def index(b, p, pt):
            return (layer, pt[b * n_pages + n_pages - 1 - (p * n_pg + k)]) + (0,) * len(shp)
        return pl.BlockSpec((1, 1) + shp, index)

    pages = ([page_spec((H_A, CHUNK, 2 * HD), k) for k in range(n_pg)] * 2
             + [page_spec((H_C, HD, CHUNK), k) for k in range(n_pg)] * 2
             + [page_spec((16, CHUNK), k) for k in range(n_pg)])
    rows_a = ROWS_A
    return pl.pallas_call(
        functools.partial(_sk3_kernel, lam_init=lam_init, n_pg=n_pg),
        grid_spec=pltpu.PrefetchScalarGridSpec(
            num_scalar_prefetch=1, grid=(nb, n_pages // n_pg),
            in_specs=[per_b((H_A, 2 * HD)), per_b((H_A, 2 * HD)), per_b((H_A, 2 * HD)),
                      per_b((H_C, HD)), per_b((H_C, HD)), per_b((HD, H_C)), per_b((HD, H_C)),
                      per_b((1, LANE)),
                      pl.BlockSpec((4, HD), lambda b, p, pt: (0, 0))] + pages,
            out_specs=[per_b((H_A, 2 * HD)), per_b((H_C, HD))],
            scratch_shapes=[pltpu.VMEM((rows_a, 2 * HD), F32), pltpu.VMEM((rows_a, 1), F32),
                            pltpu.VMEM((rows_a, 1), F32), pltpu.VMEM((rows_a, 2 * HD), F32),
                            pltpu.VMEM((H_C, HD, CHUNK), F32), pltpu.VMEM((H_C, 1), F32),
                            pltpu.VMEM((H_C, 1), F32), pltpu.VMEM((H_C, HD, CHUNK), F32),
                            pltpu.VMEM((16, 1), F32),
                            pltpu.VMEM((H_C, n_pg * CHUNK), F32), pltpu.VMEM((H_C, n_pg * CHUNK), F32)]),
        out_shape=[jax.ShapeDtypeStruct((nb, H_A, 2 * HD), F32), jax.ShapeDtypeStruct((nb, H_C, HD), F32)],
        compiler_params=_cparams("parallel", "arbitrary"),
        name="decode_paged_attention",
    )(page_table.reshape(-1), heads(qa, H_A), heads(ka, H_A), heads(va, H_A),
      qc3, kc3, jnp.swapaxes(qc3, 1, 2), jnp.swapaxes(vc3, 1, 2), lf.reshape(nb, 1, LANE), dl,
      *([cak] * n_pg), *([cav] * n_pg), *([cck] * n_pg), *([ccv] * n_pg), *([clf_t] * n_pg))


def _sk4_kernel(oa_ref, oc_ref, ub_ref, vb_ref, ws0_ref, bs0_ref, gsub_ref, gb_ref, gc_ref, w_ref,
                m_ref, cat, *, lam_init):
    j = pl.program_id(0)

    @pl.when(j == 0)
    def _():
        oa = oa_ref[...]
        parts = []
        for h in range(H_A):
            parts.append(_rms(oa[:, h * 2 * HD:(h + 1) * 2 * HD], gsub_ref[...]) * (1.0 - lam_init))
        parts.append(_rms(ub_ref[...] * (ws0_ref[...] * vb_ref[...] + bs0_ref[...]), gb_ref[...]))
        parts.append(_rms(oc_ref[...], gc_ref[...]))
        cat[...] = jnp.concatenate(parts, axis=1)

    m_ref[...] = _dot3(cat[...], w_ref[0])


def _sk4_call(layer, oa, oc, ub, vb, ws0, bs0, gsub, gb, gc, w_out, lam_init):
    nb = oa.shape[0]
    full = lambda shp: pl.BlockSpec(shp, lambda j: (0,) * len(shp))
    return pl.pallas_call(
        functools.partial(_sk4_kernel, lam_init=lam_init),
        grid=(D // TN_S,),
        in_specs=[full((nb, SEG)), full((nb, SEG)), full((nb, DB_W)), full((nb, DB_W)),
                  full((1, DB_W)), full((1, DB_W)), full((1, 2 * HD)), full((1, DB_W)), full((1, SEG)),
                  pl.BlockSpec((1, D, TN_S), lambda j: (layer, 0, j))],
        out_specs=pl.BlockSpec((nb, TN_S), lambda j: (0, j)),
        out_shape=jax.ShapeDtypeStruct((nb, D), F32),
        scratch_shapes=[pltpu.VMEM((nb, D), F32)],
        compiler_params=_cparams("arbitrary"),
        name="decode_out_proj",
    )(oa, oc, ub, vb, ws0, bs0, gsub, gb, gc, w_out)


def _sk4b_kernel(m_ref, x_ref, g1_ref, ga1_ref, g2_ref, sc2_ref, sh2_ref, wr_ref, br_ref,
                 x1_ref, h2_ref, gate_ref):
    x1 = x_ref[...] + ga1_ref[...] * _rms(m_ref[...], g1_ref[...])
    x1_ref[...] = x1
    h2 = _prenorm(x1, g2_ref[...], sc2_ref[...], sh2_ref[...])
    h2_ref[...] = h2
    e1, e2, w1, w2 = _route(_dot3(h2, wr_ref[...]) + br_ref[...])
    lane = lax.broadcasted_iota(I32, gate_ref.shape, 1).astype(F32)
    gate_ref[...] = jnp.where(lane == e1, w1, 0.0) + jnp.where(lane == e2, w2, 0.0)


def _sk4b_call(m, x, g1, ga1, g2, sc2, sh2, wr, br):
    nb = x.shape[0]
    return pl.pallas_call(
        _sk4b_kernel,
        out_shape=[jax.ShapeDtypeStruct((nb, D), F32), jax.ShapeDtypeStruct((nb, D), F32),
                   jax.ShapeDtypeStruct((nb, LANE), F32)],
        name="decode_post_router",
    )(m, x, g1, ga1, g2, sc2, sh2, wr, br)


def _sk5_kernel(el_ref, cnt_ref, h2_ref, gate_ref, x1_ref, ga_ref, g3_ref, wgu_ref, wd_ref,
                x2_ref, acc):
    s = pl.program_id(0)

    @pl.when(s == 0)
    def _():
        acc[...] = jnp.zeros_like(acc)

    @pl.when(s < cnt_ref[0])
    def _():
        hu = _dot3(h2_ref[...], wgu_ref[0, 0])
        lane = lax.broadcasted_iota(I32, gate_ref.shape, 1)
        g = jnp.sum(jnp.where(lane == el_ref[s], gate_ref[...], 0.0), axis=1, keepdims=True)
        hid = _silu(hu[:, :D_EXP]) * hu[:, D_EXP:] * g
        acc[...] += _dot3(hid, wd_ref[0, 0])

    @pl.when(s == pl.num_programs(0) - 1)
    def _():
        x2_ref[...] = x1_ref[...] + ga_ref[...] * _rms(acc[...], g3_ref[...])


def _sk5_call(layer, elist, cnt, h2, gate, x1, ga2, g3, w_gu, w_down):
    nb = h2.shape[0]
    full = lambda shp: pl.BlockSpec(shp, lambda s, el, cnt: (0,) * len(shp))
    return pl.pallas_call(
        _sk5_kernel,
        grid_spec=pltpu.PrefetchScalarGridSpec(
            num_scalar_prefetch=2, grid=(N_EXP,),
            in_specs=[full((nb, D)), full((nb, LANE)), full((nb, D)), full((nb, D)), full((1, D)),
                      pl.BlockSpec((1, 1, D, 2 * D_EXP), lambda s, el, cnt: (layer, el[s], 0, 0)),
                      pl.BlockSpec((1, 1, D_EXP, D), lambda s, el, cnt: (layer, el[s], 0, 0))],
            out_specs=full((nb, D)),
            scratch_shapes=[pltpu.VMEM((nb, D), F32)]),
        out_shape=jax.ShapeDtypeStruct((nb, D), F32),
        compiler_params=_cparams("arbitrary"),
        name="decode_moe",
    )(elist, cnt, h2, gate, x1, ga2, g3, w_gu, w_down)


def _rope_tables(pos):
    half = ROT_DIM // 2
    n = pos.shape[0]
    inv = ROPE_THETA ** (-jnp.arange(0, ROT_DIM, 2, dtype=F32) / ROT_DIM)
    ang = pos.astype(F32)[:, None] * inv[None, :]
    cos, sin = jnp.cos(ang), jnp.sin(ang)
    one = jnp.ones((n, HD - ROT_DIM), F32)
    zero = jnp.zeros((n, HD - ROT_DIM), F32)
    z8 = jnp.zeros((n, half), F32)
    c64 = jnp.concatenate([cos, cos, one], axis=1)
    sa64 = jnp.concatenate([-sin, z8, zero], axis=1)
    sb64 = jnp.concatenate([z8, sin, zero], axis=1)
    rep = lambda t: jnp.concatenate([t, t], axis=1)
    return rep(c64), rep(sa64), rep(sb64)


def _pad_lanes(a):
    return jnp.pad(a, ((0, 0), (0, LANE - a.shape[1])))


def kernel(x_prompt, x_sample, c_prompt, c_sample, cache_a_k, cache_a_v, cache_c_k, cache_c_v,
           cache_c_logf, page_table, w_ada, b_ada, g_norm, w_in, b_f, g_sgu, w_s, b_s,
           diff_lambda, g_subln, g_out_b, g_out_c, w_out, w_rg, b_rg, w_re, b_re, w_gu, w_down):
    depth = w_ada.shape[0]
    batch, seq, _ = x_prompt.shape
    nb, dec_len, _ = x_sample.shape
    assert dec_len == 1 and seq % CHUNK == 0
    n_pool, page = cache_a_k.shape[1], cache_a_k.shape[2]
    assert page == CHUNK
    past = page_table.shape[1] * page
    n = batch * seq

    rows_c = 16
    c_all = jnp.zeros((rows_c, D), F32).at[:batch].set(c_prompt).at[batch:batch + nb].set(c_sample)
    mod = _mod_call(c_all, w_ada, b_ada)

    tabs_p = _rope_tables(jnp.arange(seq))
    tabs_s = _rope_tables(jnp.full((1,), past))
    cak = jnp.transpose(cache_a_k, (0, 1, 3, 2, 4))
    cav = jnp.transpose(cache_a_v, (0, 1, 3, 2, 4))
    cck = jnp.transpose(cache_c_k, (0, 1, 3, 4, 2))
    ccv = jnp.transpose(cache_c_v, (0, 1, 3, 4, 2))
    clf_t = jnp.pad(jnp.swapaxes(cache_c_logf, 2, 3), ((0, 0), (0, 0), (0, 16 - H_C), (0, 0)))

    tq = min(TQ, seq)
    nq = seq // tq
    xp = x_prompt.reshape(n, D)
    xs = x_sample.reshape(nb, D)
    outs = [[] for _ in range(11)]
    stacked = None
    for l in range(depth):
        lam_init = 0.8 - 0.6 * math.exp(-0.3 * l)
        row = lambda v: v.reshape(1, -1)
        gn = [row(g_norm[l, k]) for k in range(4)]
        mp = [mod[l, :batch, k * D:(k + 1) * D].reshape(batch, 1, D) for k in range(6)]
        ms = [mod[l, batch:batch + nb, k * D:(k + 1) * D] for k in range(6)]
        wl = w_in[l]
        w768 = jnp.concatenate([wl[:, :OFF_B], wl[:, OFF_C:OFF_F]], axis=1).astype(BF16)
        wf = _pad_lanes(wl[:, OFF_F:])
        wb = jnp.concatenate([wl[:, OFF_B:OFF_C], wf], axis=1).astype(BF16)
        bf = _pad_lanes(row(b_f[l]))
        gs = row(g_sgu[l])
        wr = _pad_lanes(jnp.concatenate([w_rg[l], w_re[l]], axis=1))
        br = _pad_lanes(row(jnp.concatenate([b_rg[l], b_re[l]])))
        gsub, gb, gc = row(g_subln[l]), row(g_out_b[l]), row(g_out_c[l])
        dl = diff_lambda[l]

        qa, kaf, kab, vaf, vat, qc, kcf, kcb, vcf, vct = _ip1_call(
            xp, gn[0], mp[1], mp[0], w768, tabs_p, batch, seq, l, depth, stacked)
        stacked = (kaf, vaf, kcf, vcf)
        ub, vb, lfp, dcum = _ip2_call(xp, gn[0], mp[1], mp[0], wb, bf, gs, seq)
        oa = _flash_call("a", qa, kab, vat, (dl, gsub.reshape(2 * HD, 1)), batch, seq, lam_init)
        dh = (dcum[:, :H_C] * LOG2E).reshape(batch, nq, tq, H_C // 2, 2)
        dq = jnp.transpose(dh, (0, 3, 1, 4, 2))
        dk = jnp.transpose(dh, (0, 3, 1, 2, 4)).reshape(batch, H_C // 2, seq, 2)
        oc = _flash_call("c", qc, kcb, vct, (dq, dk), batch, seq)
        bst = jnp.repeat(jnp.swapaxes(b_s[l], 0, 1), HD, axis=1)
        x1, h2, eid, ew = _op_call(oa, oc, ub, vb, w_s[l], bst, gb, gc, w_out[l].astype(BF16),
                                   xp, gn[1], mp[2], gn[2], mp[4], mp[3], wr.astype(BF16), br, seq)
        tok_of_slot, tile_e, n_valid, dest = _moe_plan(eid[:, :2])
        y = _moe_call(l, tile_e, n_valid, tok_of_slot, h2, w_gu, w_down)
        xp = _combine_call(dest, y, ew, x1, mp[5], gn[3], seq)
        outs[4].append(lfp[:, :H_C])

        z, zf = _sk1_call(l, xs, gn[0], ms[1], ms[0], w_in, wf)
        qa_s, ka_s, va_s, ub_s, vb_s, qc_s, kc_s, vc_s, lf_s = _sk2_call(z, zf, tabs_s, bf, gs)
        oa_s, oc_s = _sk3_call(l, page_table, qa_s, ka_s, va_s, qc_s, kc_s, vc_s, lf_s, dl,
                               cak, cav, cck, ccv, clf_t, lam_init)
        ws0 = row(jnp.repeat(w_s[l, :, 0, 0], HD))
        bs0 = row(jnp.repeat(b_s[l, :, 0], HD))
        m_s = _sk4_call(l, oa_s.reshape(nb, SEG), oc_s.reshape(nb, SEG), ub_s, vb_s, ws0, bs0,
                        gsub, gb, gc, w_out, lam_init)
        x1_s, h2_s, gate = _sk4b_call(m_s, xs, gn[1], ms[2], gn[2], ms[4], ms[3], wr, br)
        used = jnp.any(gate[:, :N_EXP] > 0.0, axis=0)
        order = jnp.argsort(jnp.logical_not(used), stable=True).astype(I32)
        cnt = jnp.sum(used).astype(I32)
        elist = jnp.where(jnp.arange(N_EXP) < cnt, order, order[jnp.maximum(cnt - 1, 0)])
        xs = _sk5_call(l, elist, cnt.reshape(1), h2_s, gate, x1_s, ms[5], gn[3], w_gu, w_down)
        for k, a in enumerate((ka_s, va_s, kc_s, vc_s, lf_s[:, :H_C], vb_s)):
            outs[5 + k].append(a)

    st = lambda k: jnp.stack(outs[k])
    return (xp.reshape(batch, seq, D), xs.reshape(nb, 1, D),
            jnp.transpose(stacked[0], (0, 1, 3, 2, 4)), jnp.transpose(stacked[1], (0, 1, 3, 2, 4)),
            jnp.transpose(stacked[2], (0, 1, 4, 2, 3)), jnp.transpose(stacked[3], (0, 1, 4, 2, 3)),
            st(4).reshape(depth, batch, seq, H_C),
            st(5).reshape(depth, nb, 1, H_A, 2 * HD), st(6).reshape(depth, nb, 1, H_A, 2 * HD),
            st(7).reshape(depth, nb, 1, H_C, HD), st(8).reshape(depth, nb, 1, H_C, HD),
            st(9).reshape(depth, nb, 1, H_C), st(10).reshape(depth, nb, 1, G_B, HD))
```
